```python
import jax, jax.numpy as jnp
from jax import lax
import numpy as np

D_MODEL = 1024
BATCH = 32
SEQ = 2048
DEPTH = 1

N_META = 16
CHUNK = 128
EPS = 1e-6
RET_HEADS = 4
RET_DK = 256
RET_DV = 512
RET_QK = RET_HEADS * RET_DK
RET_V = RET_HEADS * RET_DV
ROPE_BASE = 10000.0
SSD_DINNER = 2 * D_MODEL
SSD_HEADDIM = 64
SSD_HEADS = SSD_DINNER // SSD_HEADDIM
SSD_GROUPS = 8
SSD_HPG = SSD_HEADS // SSD_GROUPS
SSD_STATE = 128
SSD_CONV = 4
SSD_XBC = SSD_DINNER + 2 * SSD_GROUPS * SSD_STATE
SSD_NORM_GROUP = SSD_DINNER // SSD_GROUPS
N_BRANCH = 2
D_FF = -(-8 * D_MODEL // (3 * 256)) * 256
IN_WIDTHS = (RET_QK, RET_QK, RET_V, RET_V, SSD_DINNER, SSD_XBC, SSD_HEADS, N_BRANCH * D_MODEL)
IN_WIDTH = 2 * RET_QK + 2 * RET_V + SSD_DINNER + SSD_XBC + SSD_HEADS + N_BRANCH * D_MODEL

kernel_name = 'hybrid_retention_ssd_block'


def _rmsnorm(x, g):
    xf = x.astype(jnp.float32)
    y = xf * lax.rsqrt(jnp.mean(xf * xf, axis=-1, keepdims=True) + EPS)
    return (y * g.astype(jnp.float32)).astype(x.dtype)


def _rotary(t, pos):
    half = RET_DK // 2
    inv = ROPE_BASE ** (-jnp.arange(half, dtype=jnp.float32) / half)
    ang = pos[:, None] * inv[None, :]
    cos = jnp.cos(ang)[None, :, None, :]
    sin = jnp.sin(ang)[None, :, None, :]
    t1, t2 = t[..., :half], t[..., half:]
    return jnp.concatenate([t1 * cos - t2 * sin, t1 * sin + t2 * cos], axis=-1)


def _retention(q, k, v):
    b, l = q.shape[0], q.shape[1]
    pad = CHUNK - N_META
    lp = l + pad
    nc = lp // CHUNK

    def chunks(t):
        t = jnp.pad(t, ((0, 0), (pad, 0), (0, 0), (0, 0)))
        t = t.reshape(b, nc, CHUNK, t.shape[2], t.shape[3])
        return jnp.transpose(t, (1, 0, 3, 2, 4))

    log_g = jnp.log1p(-jnp.exp2(-5.0 - jnp.arange(RET_HEADS, dtype=jnp.float32)))
    idx = jnp.arange(CHUNK, dtype=jnp.float32)
    diff = idx[:, None] - idx[None, :]
    dmask = jnp.where(diff >= 0, jnp.exp(log_g[:, None, None] * jnp.maximum(diff, 0.0)), 0.0)
    xi = jnp.exp(log_g[:, None] * (idx[None, :] + 1.0))[..., None]
    zeta = jnp.exp(log_g[:, None] * (CHUNK - 1.0 - idx[None, :]))[..., None]
    g_chunk = jnp.exp(log_g * CHUNK)[:, None, None]

    def step(state, qkv):
        qc, kc, vc = qkv
        s = jnp.einsum('bhtd,bhsd->bhts', qc, kc) * dmask
        out = (jnp.einsum('bhts,bhse->bhte', s, vc)
               + jnp.einsum('bhtd,bhde->bhte', qc, state) * xi)
        state = state * g_chunk + jnp.einsum('bhsd,bhse->bhde', kc * zeta, vc)
        return state, out

    state0 = jnp.zeros((b, RET_HEADS, RET_DK, RET_DV), jnp.float32)
    _, out = lax.scan(step, state0, (chunks(q), chunks(k), chunks(v)))
    out = jnp.transpose(out, (1, 0, 3, 2, 4)).reshape(b, lp, RET_HEADS, RET_DV)
    return out[:, pad:]


def _ssd(x, dt, bm, cm, a):
    b, l = x.shape[0], x.shape[1]
    pad = CHUNK - N_META
    lp = l + pad
    nc = lp // CHUNK

    def chunks(t):
        t = jnp.pad(t, ((0, 0), (pad, 0)) + ((0, 0),) * (t.ndim - 2))
        t = t.reshape((b, nc, CHUNK) + t.shape[2:])
        return jnp.moveaxis(t, 1, 0)

    causal = jnp.tril(jnp.ones((CHUNK, CHUNK), dtype=bool))[None, :, :, None, None]

    def step(h, inp):
        xc, dtc, bc, cc = inp
        cs = jnp.cumsum(dtc * a, axis=1)
        seg = cs[:, :, None] - cs[:, None, :]
        lmat = jnp.exp(jnp.where(causal, seg, -jnp.inf))
        cb = jnp.einsum('btgn,bsgn->btsg', cc, bc)
        y = jnp.einsum('btsg,btsgh,bsgh,bsghp->btghp', cb, lmat, dtc, xc)
        y = y + jnp.einsum('btgn,bghpn->btghp', cc, h) * jnp.exp(cs)[..., None]
        dec = jnp.exp(cs[:, -1:] - cs) * dtc
        h = (h * jnp.exp(cs[:, -1])[..., None, None]
             + jnp.einsum('bsgn,bsgh,bsghp->bghpn', bc, dec, xc))
        return h, y

    h0 = jnp.zeros((b, SSD_GROUPS, SSD_HPG, SSD_HEADDIM, SSD_STATE), jnp.float32)
    _, y = lax.scan(step, h0, (chunks(x), chunks(dt), chunks(bm), chunks(cm)))
    y = jnp.moveaxis(y, 0, 1).reshape(b, lp, SSD_GROUPS, SSD_HPG, SSD_HEADDIM)
    return y[:, pad:]


def _causal_dwconv(u, w, bias):
    out = lax.conv_general_dilated(
        u, w[:, None, :].astype(u.dtype), window_strides=(1,),
        padding=[(SSD_CONV - 1, 0)], dimension_numbers=('NWC', 'WIO', 'NWC'),
        feature_group_count=u.shape[-1])
    return out + bias


def _mixer(u, pos, w_in, conv_w, conv_b, dt_bias, a_log, d_skip, ssd_norm,
           w_ret_branch, w_ssd_branch, w_out):
    b, l = u.shape[0], u.shape[1]
    proj = u @ w_in
    offsets = np.cumsum(IN_WIDTHS)[:-1].tolist()
    q, k, v, g_ret, z, xbc, dt, gates = jnp.split(proj, offsets, axis=-1)

    q = _rotary(q.reshape(b, l, RET_HEADS, RET_DK), pos)
    k = _rotary(k.reshape(b, l, RET_HEADS, RET_DK), pos) * (RET_DK ** -0.5)
    v = v.reshape(b, l, RET_HEADS, RET_DV)
    ret = _retention(q, k, v)
    ret = ret * lax.rsqrt(jnp.mean(ret * ret, axis=-1, keepdims=True) + EPS)
    ret = ret.reshape(b, l, RET_V) * jax.nn.silu(g_ret)
    branch_a = ret @ w_ret_branch

    xbc = jax.nn.silu(_causal_dwconv(xbc, conv_w, conv_b))
    xs, bm, cm = jnp.split(xbc, [SSD_DINNER, SSD_DINNER + SSD_GROUPS * SSD_STATE], axis=-1)
    xs = xs.reshape(b, l, SSD_GROUPS, SSD_HPG, SSD_HEADDIM)
    bm = bm.reshape(b, l, SSD_GROUPS, SSD_STATE)
    cm = cm.reshape(b, l, SSD_GROUPS, SSD_STATE)
    dt = jax.nn.softplus(dt.astype(jnp.float32) + dt_bias.astype(jnp.float32))
    dt = dt.reshape(b, l, SSD_GROUPS, SSD_HPG)
    a = -jnp.exp(a_log.astype(jnp.float32)).reshape(SSD_GROUPS, SSD_HPG)
    y = _ssd(xs, dt, bm, cm, a) + d_skip.reshape(SSD_GROUPS, SSD_HPG, 1) * xs
    y = y.reshape(b, l, SSD_DINNER) * jax.nn.silu(z)
    yg = y.reshape(b, l, SSD_GROUPS, SSD_NORM_GROUP).astype(jnp.float32)
    yg = yg * lax.rsqrt(jnp.mean(yg * yg, axis=-1, keepdims=True) + EPS)
    y = yg.reshape(b, l, SSD_DINNER) * ssd_norm
    branch_b = y @ w_ssd_branch

    g_a, g_b = jnp.split(gates, N_BRANCH, axis=-1)
    merged = jax.nn.sigmoid(g_a) * branch_a + jax.nn.sigmoid(g_b) * branch_b
    return merged @ w_out


def _swiglu(u, w_gate, w_up, w_down):
    return (jax.nn.silu(u @ w_gate) * (u @ w_up)) @ w_down


def setup_inputs(seed: int = 0) -> dict:
    key = jax.random.key(seed)
    ks = jax.random.split(key, 20)
    f32 = jnp.float32

    def nrm(k, shape, scale):
        return jax.random.normal(k, shape, f32) * scale

    x = nrm(ks[0], (BATCH, SEQ, D_MODEL), 1.0)
    meta_tokens = nrm(ks[1], (N_META, D_MODEL), 1.0)
    norm_mix_pre = 1.0 + nrm(ks[2], (DEPTH, D_MODEL), 0.05)
    w_in = nrm(ks[3], (DEPTH, D_MODEL, IN_WIDTH), D_MODEL ** -0.5)
    conv_w = nrm(ks[4], (DEPTH, SSD_CONV, SSD_XBC), SSD_CONV ** -0.5)
    conv_b = nrm(ks[5], (DEPTH, SSD_XBC), 0.02)
    dt0 = jnp.exp(jax.random.uniform(ks[6], (DEPTH, SSD_HEADS), f32,
                                     minval=np.log(1e-3), maxval=np.log(1e-1)))
    dt_bias = dt0 + jnp.log(-jnp.expm1(-dt0))
    a_log = jnp.log(jax.random.uniform(ks[7], (DEPTH, SSD_HEADS), f32, minval=1.0, maxval=16.0))
    d_skip = 1.0 + nrm(ks[8], (DEPTH, SSD_HEADS), 0.05)
    ssd_norm = 1.0 + nrm(ks[9], (DEPTH, SSD_DINNER), 0.05)
    w_ret_branch = nrm(ks[10], (DEPTH, RET_V, D_MODEL), RET_V ** -0.5)
    w_ssd_branch = nrm(ks[11], (DEPTH, SSD_DINNER, D_MODEL), SSD_DINNER ** -0.5)
    w_out = nrm(ks[12], (DEPTH, D_MODEL, D_MODEL), D_MODEL ** -0.5)
    norm_mix_post = 1.0 + nrm(ks[13], (DEPTH, D_MODEL), 0.05)
    norm_ffn_pre = 1.0 + nrm(ks[14], (DEPTH, D_MODEL), 0.05)
    w_gate = nrm(ks[15], (DEPTH, D_MODEL, D_FF), D_MODEL ** -0.5)
    w_up = nrm(ks[16], (DEPTH, D_MODEL, D_FF), D_MODEL ** -0.5)
    w_down = nrm(ks[17], (DEPTH, D_FF, D_MODEL), D_FF ** -0.5)
    norm_ffn_post = 1.0 + nrm(ks[18], (DEPTH, D_MODEL), 0.05)
    return {'x': x, 'meta_tokens': meta_tokens, 'norm_mix_pre': norm_mix_pre, 'w_in': w_in,
            'conv_w': conv_w, 'conv_b': conv_b, 'dt_bias': dt_bias, 'a_log': a_log,
            'd_skip': d_skip, 'ssd_norm': ssd_norm, 'w_ret_branch': w_ret_branch,
            'w_ssd_branch': w_ssd_branch, 'w_out': w_out, 'norm_mix_post': norm_mix_post,
            'norm_ffn_pre': norm_ffn_pre, 'w_gate': w_gate, 'w_up': w_up, 'w_down': w_down,
            'norm_ffn_post': norm_ffn_post}


def reference(x, meta_tokens, norm_mix_pre, w_in, conv_w, conv_b, dt_bias, a_log, d_skip,
              ssd_norm, w_ret_branch, w_ssd_branch, w_out, norm_mix_post, norm_ffn_pre,
              w_gate, w_up, w_down, norm_ffn_post):
    b = x.shape[0]
    meta = jnp.broadcast_to(meta_tokens[None].astype(x.dtype), (b, N_META, D_MODEL))
    h = jnp.concatenate([meta, x], axis=1)
    pos = jnp.arange(h.shape[1], dtype=jnp.float32)
    for i in range(DEPTH):
        mix = _mixer(_rmsnorm(h, norm_mix_pre[i]), pos, w_in[i], conv_w[i], conv_b[i],
                     dt_bias[i], a_log[i], d_skip[i], ssd_norm[i], w_ret_branch[i],
                     w_ssd_branch[i], w_out[i])
        h = h + _rmsnorm(mix, norm_mix_post[i]).astype(h.dtype)
        f = _swiglu(_rmsnorm(h, norm_ffn_pre[i]), w_gate[i], w_up[i], w_down[i])
        h = h + _rmsnorm(f, norm_ffn_post[i]).astype(h.dtype)
    return h[:, N_META:]
```

```python
import functools

import numpy as np
import jax
import jax.numpy as jnp
from jax import lax
from jax.experimental import pallas as pl
from jax.experimental.pallas import tpu as pltpu

F32 = jnp.float32
BF16 = jnp.bfloat16

D_MODEL = 1024
N_META = 16
EPS = 1e-6
RET_HEADS = 4
RET_DK = 256
RET_DV = 512
RET_QK = RET_HEADS * RET_DK
RET_V = RET_HEADS * RET_DV
ROPE_BASE = 10000.0
SSD_DINNER = 2 * D_MODEL
SSD_HEADDIM = 64
SSD_HEADS = SSD_DINNER // SSD_HEADDIM
SSD_GROUPS = 8
SSD_HPG = SSD_HEADS // SSD_GROUPS
SSD_STATE = 128
SSD_CONV = 4
SSD_BC = SSD_GROUPS * SSD_STATE
SSD_XBC = SSD_DINNER + 2 * SSD_BC
SSD_NORM_GROUP = SSD_DINNER // SSD_GROUPS
D_FF = 2816

COL_Q = 0
COL_K = RET_QK
COL_V = 2 * RET_QK
COL_G = COL_V + RET_V
COL_Z = COL_G + RET_V
COL_XBC = COL_Z + SSD_DINNER
COL_GATES = COL_XBC + SSD_XBC
W_MAIN = COL_GATES + 2 * D_MODEL
DT_COL0 = COL_GATES
PROJ_TN = 1024
CONV_HIST = 8

VMEM_LIMIT = 56 * 1024 * 1024


def _dot(a, b):
    return jnp.dot(a, b, preferred_element_type=F32)


def _dot_nt(a, b):
    return lax.dot_general(a, b, (((1,), (1,)), ((), ())), preferred_element_type=F32)


def _dot_tn(a, b):
    return lax.dot_general(a, b, (((0,), (0,)), ((), ())), preferred_element_type=F32)


def _silu(x):
    return x * jax.nn.sigmoid(x)


def _split2(v):
    hi = v.astype(BF16)
    lo = (v - hi.astype(F32)).astype(BF16)
    return hi, lo


def _split3(v):
    hi = v.astype(BF16)
    r = v - hi.astype(F32)
    mid = r.astype(BF16)
    lo = (r - mid.astype(F32)).astype(BF16)
    return hi, mid, lo


def _proj_kernel(x_ref, g_ref, cos_ref, sin_ref, w_ref, wdt_ref, o_ref, dt_ref, hn_ref):
    j = pl.program_id(1)

    @pl.when(j == 0)
    def _():
        x = x_ref[...]
        ms = jnp.mean(x * x, axis=-1, keepdims=True)
        hn = (x * lax.rsqrt(ms + EPS) * g_ref[...]).astype(BF16)
        hn_ref[...] = hn
        dt_ref[...] = _dot(hn, wdt_ref[...])[:, :SSD_HEADS]

    acc = _dot(hn_ref[...], w_ref[...])

    @pl.when(j < 2)
    def _():
        scale = jnp.where(j == 0, 1.0, RET_DK ** -0.5).astype(F32)
        cos = cos_ref[...] * scale
        sin = sin_ref[...] * scale
        half = RET_DK // 2
        for h in range(RET_HEADS):
            lo = h * RET_DK
            t1 = acc[:, lo:lo + half]
            t2 = acc[:, lo + half:lo + RET_DK]
            o_ref[:, lo:lo + half] = (t1 * cos - t2 * sin).astype(BF16)
            o_ref[:, lo + half:lo + RET_DK] = (t1 * sin + t2 * cos).astype(BF16)

    @pl.when(j >= 2)
    def _():
        o_ref[...] = acc.astype(BF16)


def _project(x2d, gain, cos, sin, w_main, w_dt, tm):
    m = x2d.shape[0]
    rows_per_seq = cos.shape[0]
    assert m % tm == 0 and rows_per_seq % tm == 0
    pos_blocks = rows_per_seq // tm
    grid = (m // tm, W_MAIN // PROJ_TN)
    return pl.pallas_call(
        _proj_kernel,
        grid=grid,
        in_specs=[
            pl.BlockSpec((tm, D_MODEL), lambda i, j: (i, 0)),
            pl.BlockSpec((1, D_MODEL), lambda i, j: (0, 0)),
            pl.BlockSpec((tm, RET_DK // 2), lambda i, j: (i % pos_blocks, 0)),
            pl.BlockSpec((tm, RET_DK // 2), lambda i, j: (i % pos_blocks, 0)),
            pl.BlockSpec((D_MODEL, PROJ_TN), lambda i, j: (0, j)),
            pl.BlockSpec((D_MODEL, 128), lambda i, j: (0, 0)),
        ],
        out_specs=[
            pl.BlockSpec((tm, PROJ_TN), lambda i, j: (i, j)),
            pl.BlockSpec((tm, SSD_HEADS), lambda i, j: (i, 0)),
        ],
        out_shape=[
            jax.ShapeDtypeStruct((m, W_MAIN), BF16),
            jax.ShapeDtypeStruct((m, SSD_HEADS), F32),
        ],
        scratch_shapes=[pltpu.VMEM((tm, D_MODEL), BF16)],
        compiler_params=pltpu.CompilerParams(
            dimension_semantics=("parallel", "arbitrary"), vmem_limit_bytes=VMEM_LIMIT),
        name="proj",
    )(x2d, gain, cos, sin, w_main, w_dt)


def _ret_tables(t):
    log_g = np.log1p(-np.exp2(-5.0 - np.arange(RET_HEADS, dtype=np.float64)))
    idx = np.arange(t, dtype=np.float64)
    diff = idx[:, None] - idx[None, :]
    dmask = np.where(diff >= 0, np.exp(log_g[:, None, None] * np.maximum(diff, 0.0)), 0.0)
    xi = np.exp(log_g[:, None] * (idx[None, :] + 1.0))
    zeta = np.exp(log_g[:, None] * (t - 1.0 - idx[None, :]))
    g_chunk = np.exp(log_g * t)
    xi = np.broadcast_to(xi[:, :, None], (RET_HEADS, t, RET_DK))
    zeta = np.broadcast_to(zeta[:, :, None], (RET_HEADS, t, RET_DK))
    return (jnp.asarray(dmask, F32), jnp.asarray(xi, F32), jnp.asarray(zeta, F32),
            tuple(float(v) for v in g_chunk))


def _ret_kernel(*refs, g_chunk, emit_out):
    if emit_out:
        (q_ref, k_ref, v_ref, st0_ref, dmask_ref, xi_ref, zeta_ref, g_ref, w_ref,
         o_ref, st_ref, ret_ref) = refs
    else:
        (q_ref, k_ref, v_ref, st0_ref, dmask_ref, xi_ref, zeta_ref, stout_ref, st_ref) = refs
    c = pl.program_id(1)

    @pl.when(c == 0)
    def _():
        st_ref[...] = st0_ref[...]

    for h in range(RET_HEADS):
        qh = q_ref[0, :, h * RET_DK:(h + 1) * RET_DK]
        kh = k_ref[0, :, h * RET_DK:(h + 1) * RET_DK]
        vh = v_ref[0, :, h * RET_DV:(h + 1) * RET_DV]
        s = _dot_nt(qh, kh) * dmask_ref[h]
        qx = (qh.astype(F32) * xi_ref[h]).astype(BF16)
        kz = (kh.astype(F32) * zeta_ref[h]).astype(BF16)
        st = st_ref[h]
        if emit_out:
            out = _dot(s.astype(BF16), vh) + _dot(qx, st.astype(BF16))
            r = out * lax.rsqrt(jnp.mean(out * out, axis=-1, keepdims=True) + EPS)
            gate = g_ref[0, :, h * RET_DV:(h + 1) * RET_DV].astype(F32)
            ret_ref[:, h * RET_DV:(h + 1) * RET_DV] = (r * _silu(gate)).astype(BF16)
        st_ref[h] = st * g_chunk[h] + _dot_tn(kz, vh)

    if emit_out:
        o_ref[0] = _dot(ret_ref[...], w_ref[...]).astype(BF16)
    else:
        @pl.when(c == pl.num_programs(1) - 1)
        def _():
            stout_ref[...] = st_ref[...]


def _retention(proj3d, st0, w_ret, t, emit_out):
    b, l, _ = proj3d.shape
    assert l % t == 0
    nc = l // t
    dmask, xi, zeta, g_chunk = _ret_tables(t)
    const3 = lambda bb, cc: (0, 0, 0)
    in_specs = [
        pl.BlockSpec((1, t, RET_QK), lambda bb, cc: (bb, cc, COL_Q // RET_QK)),
        pl.BlockSpec((1, t, RET_QK), lambda bb, cc: (bb, cc, COL_K // RET_QK)),
        pl.BlockSpec((1, t, RET_V), lambda bb, cc: (bb, cc, COL_V // RET_V)),
        pl.BlockSpec((RET_HEADS, RET_DK, RET_DV), const3),
        pl.BlockSpec((RET_HEADS, t, t), const3),
        pl.BlockSpec((RET_HEADS, t, RET_DK), const3),
        pl.BlockSpec((RET_HEADS, t, RET_DK), const3),
    ]
    args = [proj3d, proj3d, proj3d, st0, dmask, xi, zeta]
    scratch = [pltpu.VMEM((RET_HEADS, RET_DK, RET_DV), F32)]
    if emit_out:
        in_specs += [
            pl.BlockSpec((1, t, RET_V), lambda bb, cc: (bb, cc, COL_G // RET_V)),
            pl.BlockSpec((RET_V, D_MODEL), lambda bb, cc: (0, 0)),
        ]
        args += [proj3d, w_ret]
        out_specs = pl.BlockSpec((1, t, D_MODEL), lambda bb, cc: (bb, cc, 0))
        out_shape = jax.ShapeDtypeStruct((b, l, D_MODEL), BF16)
        scratch.append(pltpu.VMEM((t, RET_V), BF16))
    else:
        assert b == 1
        out_specs = pl.BlockSpec((RET_HEADS, RET_DK, RET_DV), const3)
        out_shape = jax.ShapeDtypeStruct((RET_HEADS, RET_DK, RET_DV), F32)
    return pl.pallas_call(
        functools.partial(_ret_kernel, g_chunk=g_chunk, emit_out=emit_out),
        grid=(b, nc),
        in_specs=in_specs,
        out_specs=out_specs,
        out_shape=out_shape,
        scratch_shapes=scratch,
        compiler_params=pltpu.CompilerParams(
            dimension_semantics=("parallel", "arbitrary"), vmem_limit_bytes=VMEM_LIMIT),
        name="retention" if emit_out else "retention_meta",
    )(*args)


def _ssd_kernel(*refs, t, emit_out):
    if emit_out:
        (xbc_ref, dt_ref, hist0_ref, st0_ref, convw_ref, convb_ref, dtb_ref, alog_ref, e_ref,
         z_ref, dskip_ref, norm_ref, w_ref, o_ref, ext_ref, st_ref, y_ref) = refs
    else:
        (xbc_ref, dt_ref, hist0_ref, st0_ref, convw_ref, convb_ref, dtb_ref, alog_ref, e_ref,
         stout_ref, histout_ref, ext_ref, st_ref) = refs
    c = pl.program_id(1)

    @pl.when(c == 0)
    def _():
        ext_ref[0:CONV_HIST, :] = hist0_ref[...]
        st_ref[...] = st0_ref[...]

    ext_ref[CONV_HIST:CONV_HIST + t, :] = xbc_ref[0].astype(F32)
    acc = convb_ref[...]
    for k in range(SSD_CONV):
        off = CONV_HIST - (SSD_CONV - 1) + k
        acc = acc + convw_ref[k:k + 1, :] * ext_ref[off:off + t, :]
    ext_ref[0:CONV_HIST, :] = ext_ref[t:t + CONV_HIST, :]
    xa = _silu(acc)
    xs = xa[:, :SSD_DINNER]
    bm = xa[:, SSD_DINNER:SSD_DINNER + SSD_BC].astype(BF16)
    cm = xa[:, SSD_DINNER + SSD_BC:].astype(BF16)

    dt_raw = dt_ref[0] + dtb_ref[...]
    dt = jnp.maximum(dt_raw, 0.0) + jnp.log1p(jnp.exp(-jnp.abs(dt_raw)))
    da = dt * (-jnp.exp(alog_ref[...]))
    row = lax.broadcasted_iota(jnp.int32, (t, t), 0)
    col = lax.broadcasted_iota(jnp.int32, (t, t), 1)
    causal = row >= col
    tril = jnp.where(causal, 1.0, 0.0).astype(BF16)
    cs = sum(_dot(tril, part) for part in _split3(da))
    cs_t = cs.T
    ecs = jnp.exp(cs)
    edl = jnp.exp(cs[t - 1:t, :] - cs)

    def expand(v):
        hi, lo = _split2(v)
        return _dot(hi, e_ref[...]) + _dot(lo, e_ref[...])

    xdt_f = xs * expand(dt)
    xdt = xdt_f.astype(BF16)
    xdec = (xdt_f * expand(edl)).astype(BF16)
    ecs_e = expand(ecs)

    for g in range(SSD_GROUPS):
        gs = slice(g * SSD_NORM_GROUP, (g + 1) * SSD_NORM_GROUP)
        bg = bm[:, g * SSD_STATE:(g + 1) * SSD_STATE]
        cg = cm[:, g * SSD_STATE:(g + 1) * SSD_STATE]
        stg = st_ref[:, gs]
        if emit_out:
            cb = _dot_nt(cg, bg)
            ys = []
            for h in range(SSD_HPG):
                hh = g * SSD_HPG + h
                seg = cs[:, hh:hh + 1] - cs_t[hh:hh + 1, :]
                lmat = jnp.exp(jnp.where(causal, seg, -jnp.inf))
                m = (cb * lmat).astype(BF16)
                ys.append(_dot(m, xdt[:, hh * SSD_HEADDIM:(hh + 1) * SSD_HEADDIM]))
            y = jnp.concatenate(ys, axis=1)
            y = y + _dot(cg, stg.astype(BF16)) * ecs_e[:, gs]
            y = y + dskip_ref[:, gs] * xs[:, gs]
            y = y * _silu(z_ref[0, :, gs].astype(F32))
            y = y * lax.rsqrt(jnp.mean(y * y, axis=-1, keepdims=True) + EPS)
            y_ref[:, gs] = (y * norm_ref[:, gs]).astype(BF16)
        st_ref[:, gs] = stg * ecs_e[t - 1:t, gs] + _dot_tn(bg, xdec[:, gs])

    if emit_out:
        o_ref[0] = _dot(y_ref[...], w_ref[...]).astype(BF16)
    else:
        @pl.when(c == pl.num_programs(1) - 1)
        def _():
            stout_ref[...] = st_ref[...]
            histout_ref[...] = ext_ref[0:CONV_HIST, :]


def _ssd(proj3d, dt3d, hist0, st0, conv_w, conv_b, dt_bias, a_log, expand_mat,
         dskip_e, ssd_norm, w_ssd, t, emit_out):
    b, l, _ = proj3d.shape
    assert l % t == 0 and t >= CONV_HIST
    nc = l // t
    const2 = lambda bb, cc: (0, 0)
    in_specs = [
        pl.BlockSpec((1, t, SSD_XBC), lambda bb, cc: (bb, cc, COL_XBC // SSD_XBC)),
        pl.BlockSpec((1, t, SSD_HEADS), lambda bb, cc: (bb, cc, 0)),
        pl.BlockSpec((CONV_HIST, SSD_XBC), const2),
        pl.BlockSpec((SSD_STATE, SSD_DINNER), const2),
        pl.BlockSpec((SSD_CONV, SSD_XBC), const2),
        pl.BlockSpec((1, SSD_XBC), const2),
        pl.BlockSpec((1, SSD_HEADS), const2),
        pl.BlockSpec((1, SSD_HEADS), const2),
        pl.BlockSpec((SSD_HEADS, SSD_DINNER), const2),
    ]
    args = [proj3d, dt3d, hist0, st0, conv_w, conv_b, dt_bias, a_log, expand_mat]
    scratch = [pltpu.VMEM((CONV_HIST + t, SSD_XBC), F32), pltpu.VMEM((SSD_STATE, SSD_DINNER), F32)]
    if emit_out:
        in_specs += [
            pl.BlockSpec((1, t, SSD_DINNER), lambda bb, cc: (bb, cc, COL_Z // SSD_DINNER)),
            pl.BlockSpec((1, SSD_DINNER), const2),
            pl.BlockSpec((1, SSD_DINNER), const2),
            pl.BlockSpec((SSD_DINNER, D_MODEL), const2),
        ]
        args += [proj3d, dskip_e, ssd_norm, w_ssd]
        out_specs = pl.BlockSpec((1, t, D_MODEL), lambda bb, cc: (bb, cc, 0))
        out_shape = jax.ShapeDtypeStruct((b, l, D_MODEL), BF16)
        scratch.append(pltpu.VMEM((t, SSD_DINNER), BF16))
    else:
        assert b == 1
        out_specs = [pl.BlockSpec((SSD_STATE, SSD_DINNER), const2),
                     pl.BlockSpec((CONV_HIST, SSD_XBC), const2)]
        out_shape = [jax.ShapeDtypeStruct((SSD_STATE, SSD_DINNER), F32),
                     jax.ShapeDtypeStruct((CONV_HIST, SSD_XBC), F32)]
    return pl.pallas_call(
        functools.partial(_ssd_kernel, t=t, emit_out=emit_out),
        grid=(b, nc),
        in_specs=in_specs,
        out_specs=out_specs,
        out_shape=out_shape,
        scratch_shapes=scratch,
        compiler_params=pltpu.CompilerParams(
            dimension_semantics=("parallel", "arbitrary"), vmem_limit_bytes=VMEM_LIMIT),
        name="ssd" if emit_out else "ssd_meta",
    )(*args)


def _rms(v, gain):
    return v * lax.rsqrt(jnp.mean(v * v, axis=-1, keepdims=True) + EPS) * gain


def _tail_kernel(x_ref, ba_ref, bb_ref, gates_ref, wout_ref, npost_ref, nfpre_ref, wg_ref, wu_ref,
                 wd_ref, nfpost_ref, o_ref):
    ga = gates_ref[:, :D_MODEL].astype(F32)
    gb = gates_ref[:, D_MODEL:].astype(F32)
    merged = (jax.nn.sigmoid(ga) * ba_ref[...].astype(F32)
              + jax.nn.sigmoid(gb) * bb_ref[...].astype(F32))
    mix = _dot(merged.astype(BF16), wout_ref[...])
    h1 = x_ref[...] + _rms(mix, npost_ref[...])
    u = _rms(h1, nfpre_ref[...]).astype(BF16)
    act = (_silu(_dot(u, wg_ref[...])) * _dot(u, wu_ref[...])).astype(BF16)
    f = _dot(act, wd_ref[...])
    o_ref[...] = h1 + _rms(f, nfpost_ref[...])


def _tail(x2d, ba, bb, proj2d, w_out, n_post, nf_pre, w_gate, w_up, w_down, nf_post, tm):
    m = x2d.shape[0]
    assert m % tm == 0
    row = lambda i: (i, 0)
    const = lambda i: (0, 0)
    resident = functools.partial(pl.BlockSpec, index_map=const, pipeline_mode=pl.Buffered(1))
    return pl.pallas_call(
        _tail_kernel,
        grid=(m // tm,),
        in_specs=[
            pl.BlockSpec((tm, D_MODEL), row),
            pl.BlockSpec((tm, D_MODEL), row),
            pl.BlockSpec((tm, D_MODEL), row),
            pl.BlockSpec((tm, 2 * D_MODEL), lambda i: (i, COL_GATES // (2 * D_MODEL))),
            resident((D_MODEL, D_MODEL)),
            pl.BlockSpec((1, D_MODEL), const),
            pl.BlockSpec((1, D_MODEL), const),
            resident((D_MODEL, D_FF)),
            resident((D_MODEL, D_FF)),
            resident((D_FF, D_MODEL)),
            pl.BlockSpec((1, D_MODEL), const),
        ],
        out_specs=pl.BlockSpec((tm, D_MODEL), row),
        out_shape=jax.ShapeDtypeStruct((m, D_MODEL), F32),
        compiler_params=pltpu.CompilerParams(
            dimension_semantics=("parallel",), vmem_limit_bytes=VMEM_LIMIT),
        name="tail",
    )(x2d, ba, bb, proj2d, w_out, n_post, nf_pre, w_gate, w_up, w_down, nf_post)


PROJ_TM = 1024
RET_T = 256
SSD_T = 128
TAIL_TM = 512


def kernel(x, meta_tokens, norm_mix_pre, w_in, conv_w, conv_b, dt_bias, a_log, d_skip, ssd_norm,
           w_ret_branch, w_ssd_branch, w_out, norm_mix_post, norm_ffn_pre, w_gate, w_up, w_down,
           norm_ffn_post):
    b, seq, _ = x.shape
    assert norm_mix_pre.shape[0] == 1, "single-layer block"

    w_in0 = w_in[0]
    w_main = jnp.concatenate([w_in0[:, :DT_COL0], w_in0[:, DT_COL0 + SSD_HEADS:]], axis=1).astype(BF16)
    w_dt = jnp.pad(w_in0[:, DT_COL0:DT_COL0 + SSD_HEADS], ((0, 0), (0, 128 - SSD_HEADS))).astype(BF16)
    w_ret = w_ret_branch[0].astype(BF16)
    w_ssd = w_ssd_branch[0].astype(BF16)
    expand_mat = jnp.repeat(jnp.eye(SSD_HEADS, dtype=BF16), SSD_HEADDIM, axis=1)
    dskip_e = jnp.repeat(d_skip[0], SSD_HEADDIM)[None, :]
    gain_pre = norm_mix_pre[0][None, :]
    conv_b2 = conv_b[0][None, :]
    dt_bias2 = dt_bias[0][None, :]
    a_log2 = a_log[0][None, :]
    ssd_norm2 = ssd_norm[0][None, :]

    half = RET_DK // 2
    inv = ROPE_BASE ** (-jnp.arange(half, dtype=F32) / half)
    ang = jnp.arange(N_META + seq, dtype=F32)[:, None] * inv[None, :]
    cos, sin = jnp.cos(ang), jnp.sin(ang)

    proj_m, dt_m = _project(meta_tokens.astype(F32), gain_pre, cos[:N_META], sin[:N_META],
                            w_main, w_dt, N_META)
    proj_m = proj_m[None]
    ret_st = _retention(proj_m, jnp.zeros((RET_HEADS, RET_DK, RET_DV), F32), None, N_META, False)
    ssd_st, hist = _ssd(proj_m, dt_m[None], jnp.zeros((CONV_HIST, SSD_XBC), F32),
                        jnp.zeros((SSD_STATE, SSD_DINNER), F32), conv_w[0], conv_b2, dt_bias2,
                        a_log2, expand_mat, None, None, None, N_META, False)

    x2d = x.reshape(b * seq, D_MODEL)
    proj, dt = _project(x2d, gain_pre, cos[N_META:], sin[N_META:], w_main, w_dt, PROJ_TM)
    proj3d = proj.reshape(b, seq, W_MAIN)
    branch_a = _retention(proj3d, ret_st, w_ret, RET_T, True)
    branch_b = _ssd(proj3d, dt.reshape(b, seq, SSD_HEADS), hist, ssd_st, conv_w[0], conv_b2,
                    dt_bias2, a_log2, expand_mat, dskip_e, ssd_norm2, w_ssd, SSD_T, True)
    out = _tail(x2d, branch_a.reshape(b * seq, D_MODEL), branch_b.reshape(b * seq, D_MODEL), proj,
                w_out[0].astype(BF16), norm_mix_post[0][None, :], norm_ffn_pre[0][None, :],
                w_gate[0].astype(BF16), w_up[0].astype(BF16), w_down[0].astype(BF16),
                norm_ffn_post[0][None, :], TAIL_TM)
    return out.reshape(b, seq, D_MODEL)
```

```python
import functools

import numpy as np
import jax
import jax.numpy as jnp
from jax import lax
from jax.experimental import pallas as pl
from jax.experimental.pallas import tpu as pltpu

F32 = jnp.float32
BF16 = jnp.bfloat16

D_MODEL = 1024
N_META = 16
EPS = 1e-6
RET_HEADS = 4
RET_DK = 256
RET_DV = 512
RET_QK = RET_HEADS * RET_DK
RET_V = RET_HEADS * RET_DV
ROPE_BASE = 10000.0
SSD_DINNER = 2 * D_MODEL
SSD_HEADDIM = 64
SSD_HEADS = SSD_DINNER // SSD_HEADDIM
SSD_GROUPS = 8
SSD_HPG = SSD_HEADS // SSD_GROUPS
SSD_STATE = 128
SSD_CONV = 4
SSD_BC = SSD_GROUPS * SSD_STATE
SSD_XBC = SSD_DINNER + 2 * SSD_BC
SSD_NORM_GROUP = SSD_DINNER // SSD_GROUPS
D_FF = 2816
LOG2E = 1.4426950408889634

COL_Q = 0
COL_K = RET_QK
COL_V = 2 * RET_QK
COL_G = COL_V + RET_V
COL_Z = COL_G + RET_V
COL_XBC = COL_Z + SSD_DINNER
COL_GATES = COL_XBC + SSD_XBC
W_MAIN = COL_GATES + 2 * D_MODEL
DT_COL0 = COL_GATES
PROJ_TN = 1024
HALO = 16
DT_W = 2 * SSD_HEADS

VMEM_LIMIT = 56 * 1024 * 1024


def _dot(a, b):
    return jnp.dot(a, b, preferred_element_type=F32)


def _dot_nt(a, b):
    return lax.dot_general(a, b, (((1,), (1,)), ((), ())), preferred_element_type=F32)


def _dot_tn(a, b):
    return lax.dot_general(a, b, (((0,), (0,)), ((), ())), preferred_element_type=F32)


def _silu(x):
    return x * jax.nn.sigmoid(x)


def _split2(v):
    hi = v.astype(BF16)
    lo = (v - hi.astype(F32)).astype(BF16)
    return hi, lo


def _split3(v):
    hi = v.astype(BF16)
    r = v - hi.astype(F32)
    mid = r.astype(BF16)
    lo = (r - mid.astype(F32)).astype(BF16)
    return hi, mid, lo


def _rms(v, gain):
    return v * lax.rsqrt(jnp.mean(v * v, axis=-1, keepdims=True) + EPS) * gain


def _proj_kernel(x_ref, halo_ref, meta_ref, g_ref, cos_ref, sin_ref, w_ref, wdt_ref, cw_ref, cb_ref,
                 o_ref, dt_ref, hn_ref, *, tm, rb, seq_tiles):
    i = pl.program_id(0)
    j = pl.program_id(1)
    jq, jk, jv, jg, jz, jx, jgt = (c // PROJ_TN for c in
                                   (COL_Q, COL_K, COL_V, COL_G, COL_Z, COL_XBC, COL_GATES))

    @pl.when(j == 0)
    def _():
        first = (i % seq_tiles) == 0
        left = jnp.where(first, meta_ref[...], halo_ref[...])
        hn_ref[0:HALO, :] = _rms(left, g_ref[...]).astype(BF16)
        hn = _rms(x_ref[...], g_ref[...]).astype(BF16)
        hn_ref[HALO:, :] = hn
        dt_ref[...] = _dot(hn, wdt_ref[...])[:, :DT_W]

    nrb = tm // rb

    @pl.when(j < jv)
    def _():
        scale = jnp.where(j == jq, 1.0, RET_DK ** -0.5).astype(F32)
        half = RET_DK // 2
        for r in range(nrb):
            rows = slice(r * rb, (r + 1) * rb)
            acc = _dot(hn_ref[HALO + r * rb:HALO + (r + 1) * rb, :], w_ref[...])
            cos = cos_ref[rows, :] * scale
            sin = sin_ref[rows, :] * scale
            for h in range(RET_HEADS):
                lo = h * RET_DK
                t1 = acc[:, lo:lo + half]
                t2 = acc[:, lo + half:lo + RET_DK]
                o_ref[rows, lo:lo + half] = (t1 * cos - t2 * sin).astype(BF16)
                o_ref[rows, lo + half:lo + RET_DK] = (t1 * sin + t2 * cos).astype(BF16)

    @pl.when((j >= jv) & ((j < jx) | (j >= jgt)))
    def _():
        is_v = j < jg
        is_gate = j >= jgt
        for r in range(nrb):
            acc = _dot(hn_ref[HALO + r * rb:HALO + (r + 1) * rb, :], w_ref[...])
            s = jax.nn.sigmoid(acc)
            out = jnp.where(is_v, acc, jnp.where(is_gate, s, acc * s))
            o_ref[r * rb:(r + 1) * rb, :] = out.astype(BF16)

    @pl.when((j >= jx) & (j < jgt))
    def _():
        n = rb + 8
        for r in range(nrb):
            acc = _dot(hn_ref[r * rb:r * rb + rb + HALO, :], w_ref[...])
            full = acc[HALO - 8:, :]
            out = cb_ref[...] + cw_ref[SSD_CONV - 1:SSD_CONV, :] * full[8:, :]
            for k in range(SSD_CONV - 1):
                back = SSD_CONV - 1 - k
                win = pltpu.roll(full, n - (8 - back), axis=0)[:rb, :]
                out = out + cw_ref[k:k + 1, :] * win
            o_ref[r * rb:(r + 1) * rb, :] = _silu(out).astype(BF16)


def _project(x2d, left_ctx, gain, cos, sin, w_main, w_dt, conv_w, conv_b, tm, rb):
    m = x2d.shape[0]
    rows_per_seq = cos.shape[0]
    assert m % tm == 0 and rows_per_seq % tm == 0 and tm % rb == 0 and tm % HALO == 0
    seq_tiles = rows_per_seq // tm
    grid = (m // tm, W_MAIN // PROJ_TN)
    jx = COL_XBC // PROJ_TN
    nx = SSD_XBC // PROJ_TN
    conv_col = lambda i, j: (0, jnp.clip(j - jx, 0, nx - 1))
    return pl.pallas_call(
        functools.partial(_proj_kernel, tm=tm, rb=rb, seq_tiles=seq_tiles),
        grid=grid,
        in_specs=[
            pl.BlockSpec((tm, D_MODEL), lambda i, j: (i, 0)),
            pl.BlockSpec((HALO, D_MODEL), lambda i, j: (jnp.maximum(i * (tm // HALO) - 1, 0), 0)),
            pl.BlockSpec((HALO, D_MODEL), lambda i, j: (0, 0)),
            pl.BlockSpec((1, D_MODEL), lambda i, j: (0, 0)),
            pl.BlockSpec((tm, RET_DK // 2), lambda i, j: (i % seq_tiles, 0)),
            pl.BlockSpec((tm, RET_DK // 2), lambda i, j: (i % seq_tiles, 0)),
            pl.BlockSpec((D_MODEL, PROJ_TN), lambda i, j: (0, j)),
            pl.BlockSpec((D_MODEL, 128), lambda i, j: (0, 0)),
            pl.BlockSpec((SSD_CONV, PROJ_TN), conv_col),
            pl.BlockSpec((1, PROJ_TN), conv_col),
        ],
        out_specs=[
            pl.BlockSpec((tm, PROJ_TN), lambda i, j: (i, j)),
            pl.BlockSpec((tm, DT_W), lambda i, j: (i, 0)),
        ],
        out_shape=[
            jax.ShapeDtypeStruct((m, W_MAIN), BF16),
            jax.ShapeDtypeStruct((m, DT_W), F32),
        ],
        scratch_shapes=[pltpu.VMEM((HALO + tm, D_MODEL), BF16)],
        compiler_params=pltpu.CompilerParams(
            dimension_semantics=("parallel", "arbitrary"), vmem_limit_bytes=VMEM_LIMIT),
        name="proj",
    )(x2d, x2d, left_ctx, gain, cos, sin, w_main, w_dt, conv_w, conv_b)


def _ret_tables(t):
    log_g = np.log1p(-np.exp2(-5.0 - np.arange(RET_HEADS, dtype=np.float64)))
    idx = np.arange(t, dtype=np.float64)
    diff = idx[:, None] - idx[None, :]
    dmask = np.where(diff >= 0, np.exp(log_g[:, None, None] * np.maximum(diff, 0.0)), 0.0)
    xi = np.exp(log_g[:, None] * (idx[None, :] + 1.0))
    zeta = np.exp(log_g[:, None] * (t - 1.0 - idx[None, :]))
    g_chunk = np.exp(log_g * t)
    xi = np.broadcast_to(xi[:, :, None], (RET_HEADS, t, RET_DK))
    zeta = np.broadcast_to(zeta[:, :, None], (RET_HEADS, t, RET_DK))
    return (jnp.asarray(dmask, F32), jnp.asarray(xi, F32), jnp.asarray(zeta, F32),
            tuple(float(v) for v in g_chunk))


def _ret_kernel(*refs, g_chunk, emit_out):
    if emit_out:
        (q_ref, k_ref, v_ref, st0_ref, dmask_ref, xi_ref, zeta_ref, g_ref, w_ref,
         o_ref, st_ref, ret_ref) = refs
    else:
        (q_ref, k_ref, v_ref, st0_ref, dmask_ref, xi_ref, zeta_ref, stout_ref, st_ref) = refs
    c = pl.program_id(1)

    @pl.when(c == 0)
    def _():
        st_ref[...] = st0_ref[...]

    for h in range(RET_HEADS):
        qh = q_ref[0, :, h * RET_DK:(h + 1) * RET_DK]
        kh = k_ref[0, :, h * RET_DK:(h + 1) * RET_DK]
        vh = v_ref[0, :, h * RET_DV:(h + 1) * RET_DV]
        s = _dot_nt(qh, kh) * dmask_ref[h]
        qx = (qh.astype(F32) * xi_ref[h]).astype(BF16)
        kz = (kh.astype(F32) * zeta_ref[h]).astype(BF16)
        st = st_ref[h]
        if emit_out:
            out = _dot(s.astype(BF16), vh) + _dot(qx, st.astype(BF16))
            r = out * lax.rsqrt(jnp.mean(out * out, axis=-1, keepdims=True) + EPS)
            gate = g_ref[0, :, h * RET_DV:(h + 1) * RET_DV].astype(F32)
            ret_ref[:, h * RET_DV:(h + 1) * RET_DV] = (r * gate).astype(BF16)
        st_ref[h] = st * g_chunk[h] + _dot_tn(kz, vh)

    if emit_out:
        o_ref[0] = _dot(ret_ref[...], w_ref[...]).astype(BF16)
    else:
        @pl.when(c == pl.num_programs(1) - 1)
        def _():
            stout_ref[...] = st_ref[...]


def _retention(proj3d, st0, w_ret, t, emit_out):
    b, l, _ = proj3d.shape
    assert l % t == 0
    nc = l // t
    dmask, xi, zeta, g_chunk = _ret_tables(t)
    const3 = lambda bb, cc: (0, 0, 0)
    in_specs = [
        pl.BlockSpec((1, t, RET_QK), lambda bb, cc: (bb, cc, COL_Q // RET_QK)),
        pl.BlockSpec((1, t, RET_QK), lambda bb, cc: (bb, cc, COL_K // RET_QK)),
        pl.BlockSpec((1, t, RET_V), lambda bb, cc: (bb, cc, COL_V // RET_V)),
        pl.BlockSpec((RET_HEADS, RET_DK, RET_DV), const3),
        pl.BlockSpec((RET_HEADS, t, t), const3),
        pl.BlockSpec((RET_HEADS, t, RET_DK), const3),
        pl.BlockSpec((RET_HEADS, t, RET_DK), const3),
    ]
    args = [proj3d, proj3d, proj3d, st0, dmask, xi, zeta]
    scratch = [pltpu.VMEM((RET_HEADS, RET_DK, RET_DV), F32)]
    if emit_out:
        in_specs += [
            pl.BlockSpec((1, t, RET_V), lambda bb, cc: (bb, cc, COL_G // RET_V)),
            pl.BlockSpec((RET_V, D_MODEL), lambda bb, cc: (0, 0)),
        ]
        args += [proj3d, w_ret]
        out_specs = pl.BlockSpec((1, t, D_MODEL), lambda bb, cc: (bb, cc, 0))
        out_shape = jax.ShapeDtypeStruct((b, l, D_MODEL), BF16)
        scratch.append(pltpu.VMEM((t, RET_V), BF16))
    else:
        assert b == 1
        out_specs = pl.BlockSpec((RET_HEADS, RET_DK, RET_DV), const3)
        out_shape = jax.ShapeDtypeStruct((RET_HEADS, RET_DK, RET_DV), F32)
    return pl.pallas_call(
        functools.partial(_ret_kernel, g_chunk=g_chunk, emit_out=emit_out),
        grid=(b, nc),
        in_specs=in_specs,
        out_specs=out_specs,
        out_shape=out_shape,
        scratch_shapes=scratch,
        compiler_params=pltpu.CompilerParams(
            dimension_semantics=("parallel", "arbitrary"), vmem_limit_bytes=VMEM_LIMIT),
        name="retention" if emit_out else "retention_meta",
    )(*args)


def _ssd_kernel(*refs, t, emit_out):
    if emit_out:
        (xbc_ref, dt_ref, st0_ref, dtb_ref, alog_ref, e_ref,
         z_ref, dskip_ref, norm_ref, w_ref, o_ref, st_ref, y_ref) = refs
    else:
        (xbc_ref, dt_ref, st0_ref, dtb_ref, alog_ref, e_ref, stout_ref, st_ref) = refs
    c = pl.program_id(1)

    @pl.when(c == 0)
    def _():
        st_ref[...] = st0_ref[...]

    dt_raw = dt_ref[0] + dtb_ref[...]
    dt = jnp.maximum(dt_raw, 0.0) + jnp.log1p(jnp.exp(-jnp.abs(dt_raw)))
    da = dt * (-jnp.exp(alog_ref[...]))
    row = lax.broadcasted_iota(jnp.int32, (t, t), 0)
    col = lax.broadcasted_iota(jnp.int32, (t, t), 1)
    causal = row >= col
    tril = jnp.where(causal, 1.0, 0.0).astype(BF16)
    cs = sum(_dot(tril, part) for part in _split3(da))
    ecs = jnp.exp(cs)
    edl = jnp.exp(cs[t - 1:t, :] - cs)
    cs2 = cs * LOG2E
    cs2_t = cs2.T

    lane = lax.broadcasted_iota(jnp.int32, (t, DT_W), 1)

    def expand(v):
        hi, lo = _split2(v)
        return _dot(jnp.where(lane < SSD_HEADS, hi, lo), e_ref[...])

    xs = xbc_ref[0, :, :SSD_DINNER].astype(F32)
    xdt_f = xs * expand(dt)
    xdt = xdt_f.astype(BF16)
    xdec = (xdt_f * expand(edl)).astype(BF16)
    ecs_e = expand(ecs)

    for g in range(SSD_GROUPS):
        gs = slice(g * SSD_NORM_GROUP, (g + 1) * SSD_NORM_GROUP)
        bg = xbc_ref[0, :, SSD_DINNER + g * SSD_STATE:SSD_DINNER + (g + 1) * SSD_STATE]
        cg = xbc_ref[0, :, SSD_DINNER + SSD_BC + g * SSD_STATE:SSD_DINNER + SSD_BC + (g + 1) * SSD_STATE]
        stg = st_ref[:, gs]
        if emit_out:
            cb = jnp.where(causal, _dot_nt(cg, bg), 0.0)
            ys = []
            for h in range(SSD_HPG):
                hh = g * SSD_HPG + h
                seg = jnp.minimum(cs2[:, hh:hh + 1] - cs2_t[hh:hh + 1, :], 0.0)
                m = (cb * jnp.exp2(seg)).astype(BF16)
                ys.append(_dot(m, xdt[:, hh * SSD_HEADDIM:(hh + 1) * SSD_HEADDIM]))
            y = jnp.concatenate(ys, axis=1)
            y = y + _dot(cg, stg.astype(BF16)) * ecs_e[:, gs]
            y = y + dskip_ref[:, gs] * xs[:, gs]
            y = y * z_ref[0, :, gs].astype(F32)
            y = y * lax.rsqrt(jnp.mean(y * y, axis=-1, keepdims=True) + EPS)
            y_ref[:, gs] = (y * norm_ref[:, gs]).astype(BF16)
        st_ref[:, gs] = stg * ecs_e[t - 1:t, gs] + _dot_tn(bg, xdec[:, gs])

    if emit_out:
        o_ref[0] = _dot(y_ref[...], w_ref[...]).astype(BF16)
    else:
        @pl.when(c == pl.num_programs(1) - 1)
        def _():
            stout_ref[...] = st_ref[...]


def _ssd(proj3d, dt3d, st0, dt_bias, a_log, expand_mat, dskip_e, ssd_norm, w_ssd, t, emit_out):
    b, l, _ = proj3d.shape
    assert l % t == 0
    nc = l // t
    const2 = lambda bb, cc: (0, 0)
    in_specs = [
        pl.BlockSpec((1, t, SSD_XBC), lambda bb, cc: (bb, cc, COL_XBC // SSD_XBC)),
        pl.BlockSpec((1, t, DT_W), lambda bb, cc: (bb, cc, 0)),
        pl.BlockSpec((SSD_STATE, SSD_DINNER), const2),
        pl.BlockSpec((1, DT_W), const2),
        pl.BlockSpec((1, DT_W), const2),
        pl.BlockSpec((DT_W, SSD_DINNER), const2),
    ]
    args = [proj3d, dt3d, st0, dt_bias, a_log, expand_mat]
    scratch = [pltpu.VMEM((SSD_STATE, SSD_DINNER), F32)]
    if emit_out:
        in_specs += [
            pl.BlockSpec((1, t, SSD_DINNER), lambda bb, cc: (bb, cc, COL_Z // SSD_DINNER)),
            pl.BlockSpec((1, SSD_DINNER), const2),
            pl.BlockSpec((1, SSD_DINNER), const2),
            pl.BlockSpec((SSD_DINNER, D_MODEL), const2),
        ]
        args += [proj3d, dskip_e, ssd_norm, w_ssd]
        out_specs = pl.BlockSpec((1, t, D_MODEL), lambda bb, cc: (bb, cc, 0))
        out_shape = jax.ShapeDtypeStruct((b, l, D_MODEL), BF16)
        scratch.append(pltpu.VMEM((t, SSD_DINNER), BF16))
    else:
        assert b == 1
        out_specs = pl.BlockSpec((SSD_STATE, SSD_DINNER), const2)
        out_shape = jax.ShapeDtypeStruct((SSD_STATE, SSD_DINNER), F32)
    return pl.pallas_call(
        functools.partial(_ssd_kernel, t=t, emit_out=emit_out),
        grid=(b, nc),
        in_specs=in_specs,
        out_specs=out_specs,
        out_shape=out_shape,
        scratch_shapes=scratch,
        compiler_params=pltpu.CompilerParams(
            dimension_semantics=("parallel", "arbitrary"), vmem_limit_bytes=VMEM_LIMIT),
        name="ssd" if emit_out else "ssd_meta",
    )(*args)


def _tail_kernel(x_ref, ba_ref, bb_ref, gates_ref, wout_ref, npost_ref, nfpre_ref, wg_ref, wu_ref,
                 wd_ref, nfpost_ref, o_ref):
    ga = gates_ref[:, :D_MODEL].astype(F32)
    gb = gates_ref[:, D_MODEL:].astype(F32)
    merged = ga * ba_ref[...].astype(F32) + gb * bb_ref[...].astype(F32)
    mix = _dot(merged.astype(BF16), wout_ref[...])
    h1 = x_ref[...] + _rms(mix, npost_ref[...])
    u = _rms(h1, nfpre_ref[...]).astype(BF16)
    act = (_silu(_dot(u, wg_ref[...])) * _dot(u, wu_ref[...])).astype(BF16)
    f = _dot(act, wd_ref[...])
    o_ref[...] = h1 + _rms(f, nfpost_ref[...])


def _tail(x2d, ba, bb, proj2d, w_out, n_post, nf_pre, w_gate, w_up, w_down, nf_post, tm):
    m = x2d.shape[0]
    assert m % tm == 0
    row = lambda i: (i, 0)
    const = lambda i: (0, 0)
    resident = functools.partial(pl.BlockSpec, index_map=const, pipeline_mode=pl.Buffered(1))
    return pl.pallas_call(
        _tail_kernel,
        grid=(m // tm,),
        in_specs=[
            pl.BlockSpec((tm, D_MODEL), row),
            pl.BlockSpec((tm, D_MODEL), row),
            pl.BlockSpec((tm, D_MODEL), row),
            pl.BlockSpec((tm, 2 * D_MODEL), lambda i: (i, COL_GATES // (2 * D_MODEL))),
            resident((D_MODEL, D_MODEL)),
            pl.BlockSpec((1, D_MODEL), const),
            pl.BlockSpec((1, D_MODEL), const),
            resident((D_MODEL, D_FF)),
            resident((D_MODEL, D_FF)),
            resident((D_FF, D_MODEL)),
            pl.BlockSpec((1, D_MODEL), const),
        ],
        out_specs=pl.BlockSpec((tm, D_MODEL), row),
        out_shape=jax.ShapeDtypeStruct((m, D_MODEL), F32),
        compiler_params=pltpu.CompilerParams(
            dimension_semantics=("parallel",), vmem_limit_bytes=VMEM_LIMIT),
        name="tail",
    )(x2d, ba, bb, proj2d, w_out, n_post, nf_pre, w_gate, w_up, w_down, nf_post)


PROJ_TM = 1024
PROJ_RB = 256
RET_T = 256
SSD_T = 128
TAIL_TM = 512


def kernel(x, meta_tokens, norm_mix_pre, w_in, conv_w, conv_b, dt_bias, a_log, d_skip, ssd_norm,
           w_ret_branch, w_ssd_branch, w_out, norm_mix_post, norm_ffn_pre, w_gate, w_up, w_down,
           norm_ffn_post):
    b, seq, _ = x.shape
    assert norm_mix_pre.shape[0] == 1, "single-layer block"
    assert meta_tokens.shape[0] == N_META == HALO

    w_in0 = w_in[0]
    w_main = jnp.concatenate([w_in0[:, :DT_COL0], w_in0[:, DT_COL0 + SSD_HEADS:]], axis=1).astype(BF16)
    w_dt1 = w_in0[:, DT_COL0:DT_COL0 + SSD_HEADS]
    w_dt = jnp.pad(jnp.concatenate([w_dt1, w_dt1], axis=1), ((0, 0), (0, 128 - DT_W))).astype(BF16)
    w_ret = w_ret_branch[0].astype(BF16)
    w_ssd = w_ssd_branch[0].astype(BF16)
    expand_mat = jnp.tile(jnp.repeat(jnp.eye(SSD_HEADS, dtype=BF16), SSD_HEADDIM, axis=1), (2, 1))
    dskip_e = jnp.repeat(d_skip[0], SSD_HEADDIM)[None, :]
    gain_pre = norm_mix_pre[0][None, :]
    conv_b2 = conv_b[0][None, :]
    dt_bias2 = jnp.tile(dt_bias[0], 2)[None, :]
    a_log2 = jnp.tile(a_log[0], 2)[None, :]
    ssd_norm2 = ssd_norm[0][None, :]
    meta = meta_tokens.astype(F32)

    half = RET_DK // 2
    inv = ROPE_BASE ** (-jnp.arange(half, dtype=F32) / half)
    ang = jnp.arange(N_META + seq, dtype=F32)[:, None] * inv[None, :]
    cos, sin = jnp.cos(ang), jnp.sin(ang)

    proj_m, dt_m = _project(meta, jnp.zeros((HALO, D_MODEL), F32), gain_pre, cos[:N_META],
                            sin[:N_META], w_main, w_dt, conv_w[0], conv_b2, N_META, N_META)
    proj_m = proj_m[None]
    ret_st = _retention(proj_m, jnp.zeros((RET_HEADS, RET_DK, RET_DV), F32), None, N_META, False)
    ssd_st = _ssd(proj_m, dt_m[None], jnp.zeros((SSD_STATE, SSD_DINNER), F32), dt_bias2, a_log2,
                  expand_mat, None, None, None, N_META, False)

    x2d = x.reshape(b * seq, D_MODEL)
    proj, dt = _project(x2d, meta, gain_pre, cos[N_META:], sin[N_META:], w_main, w_dt, conv_w[0],
                        conv_b2, PROJ_TM, PROJ_RB)
    proj3d = proj.reshape(b, seq, W_MAIN)
    branch_a = _retention(proj3d, ret_st, w_ret, RET_T, True)
    branch_b = _ssd(proj3d, dt.reshape(b, seq, DT_W), ssd_st, dt_bias2, a_log2, expand_mat,
                    dskip_e, ssd_norm2, w_ssd, SSD_T, True)
    out = _tail(x2d, branch_a.reshape(b * seq, D_MODEL), branch_b.reshape(b * seq, D_MODEL), proj,
                w_out[0].astype(BF16), norm_mix_post[0][None, :], norm_ffn_pre[0][None, :],
                w_gate[0].astype(BF16), w_up[0].astype(BF16), w_down[0].astype(BF16),
                norm_ffn_post[0][None, :], TAIL_TM)
    return out.reshape(b, seq, D_MODEL)
```

```python
import functools

import numpy as np
import jax
import jax.numpy as jnp
from jax import lax
from jax.experimental import pallas as pl
from jax.experimental.pallas import tpu as pltpu

F32 = jnp.float32
BF16 = jnp.bfloat16

D_MODEL = 1024
N_META = 16
EPS = 1e-6
RET_HEADS = 4
RET_DK = 256
RET_DV = 512
RET_QK = RET_HEADS * RET_DK
RET_V = RET_HEADS * RET_DV
ROPE_BASE = 10000.0
SSD_DINNER = 2 * D_MODEL
SSD_HEADDIM = 64
SSD_HEADS = SSD_DINNER // SSD_HEADDIM
SSD_GROUPS = 8
SSD_HPG = SSD_HEADS // SSD_GROUPS
SSD_STATE = 128
SSD_CONV = 4
SSD_BC = SSD_GROUPS * SSD_STATE
SSD_XBC = SSD_DINNER + 2 * SSD_BC
SSD_NORM_GROUP = SSD_DINNER // SSD_GROUPS
D_FF = 2816
LOG2E = 1.4426950408889634

COL_Q = 0
COL_K = RET_QK
COL_V = 2 * RET_QK
COL_G = COL_V + RET_V
COL_Z = COL_G + RET_V
COL_XBC = COL_Z + SSD_DINNER
COL_DT = COL_XBC + SSD_XBC
COL_GATES = COL_DT + SSD_HEADS
PROJ_TN = 1024
HALO = 16
DT_W = 2 * SSD_HEADS

VMEM_LIMIT = 56 * 1024 * 1024


def _dot(a, b):
    return jnp.dot(a, b, preferred_element_type=F32)


def _dot_nt(a, b):
    return lax.dot_general(a, b, (((1,), (1,)), ((), ())), preferred_element_type=F32)


def _dot_tn(a, b):
    return lax.dot_general(a, b, (((0,), (0,)), ((), ())), preferred_element_type=F32)


def _silu(x):
    return x * jax.nn.sigmoid(x)


def _split2(v):
    hi = v.astype(BF16)
    lo = (v - hi.astype(F32)).astype(BF16)
    return hi, lo


def _split3(v):
    hi = v.astype(BF16)
    r = v - hi.astype(F32)
    mid = r.astype(BF16)
    lo = (r - mid.astype(F32)).astype(BF16)
    return hi, mid, lo


def _rms(v, gain):
    return v * lax.rsqrt(jnp.mean(v * v, axis=-1, keepdims=True) + EPS) * gain


def _params(*sem):
    return pltpu.CompilerParams(dimension_semantics=sem, vmem_limit_bytes=VMEM_LIMIT)


def _norm_kernel(x_ref, g_ref, wdt_ref, hn_ref, dt_ref):
    hn = _rms(x_ref[...], g_ref[...]).astype(BF16)
    hn_ref[...] = hn
    dt_ref[...] = _dot(hn, wdt_ref[...])[:, :DT_W]


def _prenorm(x2d, gain, w_dt, tm):
    m = x2d.shape[0]
    assert m % tm == 0
    return pl.pallas_call(
        _norm_kernel,
        grid=(m // tm,),
        in_specs=[
            pl.BlockSpec((tm, D_MODEL), lambda i: (i, 0)),
            pl.BlockSpec((1, D_MODEL), lambda i: (0, 0)),
            pl.BlockSpec((D_MODEL, 128), lambda i: (0, 0)),
        ],
        out_specs=[
            pl.BlockSpec((tm, D_MODEL), lambda i: (i, 0)),
            pl.BlockSpec((tm, DT_W), lambda i: (i, 0)),
        ],
        out_shape=[jax.ShapeDtypeStruct((m, D_MODEL), BF16), jax.ShapeDtypeStruct((m, DT_W), F32)],
        compiler_params=_params("parallel"),
        name="prenorm",
    )(x2d, gain, w_dt)


def _lin_kernel(*refs, kind, tm, seq_tiles):
    if kind == "rotary":
        hn_ref, w_ref, cos_ref, sin_ref, o_ref = refs
        scale = jnp.where(pl.program_id(1) == 0, 1.0, RET_DK ** -0.5).astype(F32)
        acc = _dot(hn_ref[...], w_ref[...])
        cos = cos_ref[...] * scale
        sin = sin_ref[...] * scale
        half = RET_DK // 2
        for h in range(RET_HEADS):
            lo = h * RET_DK
            t1 = acc[:, lo:lo + half]
            t2 = acc[:, lo + half:lo + RET_DK]
            o_ref[:, lo:lo + half] = (t1 * cos - t2 * sin).astype(BF16)
            o_ref[:, lo + half:lo + RET_DK] = (t1 * sin + t2 * cos).astype(BF16)
    elif kind == "conv":
        hn_ref, w_ref, halo_ref, meta_ref, cw_ref, cb_ref, o_ref = refs
        first = (pl.program_id(0) % seq_tiles) == 0
        left = jnp.where(first, meta_ref[...], halo_ref[...])
        acc = _dot(jnp.concatenate([left, hn_ref[...]], axis=0), w_ref[...])
        full = acc[HALO - 8:, :]
        n = tm + 8
        out = cb_ref[...] + cw_ref[SSD_CONV - 1:SSD_CONV, :] * full[8:, :]
        for k in range(SSD_CONV - 1):
            back = SSD_CONV - 1 - k
            win = pltpu.roll(full, n - (8 - back), axis=0)[:tm, :]
            out = out + cw_ref[k:k + 1, :] * win
        o_ref[...] = _silu(out).astype(BF16)
    else:
        hn_ref, w_ref, o_ref = refs
        acc = _dot(hn_ref[...], w_ref[...])
        if kind == "silu":
            acc = _silu(acc)
        elif kind == "sigmoid":
            acc = jax.nn.sigmoid(acc)
        else:
            assert kind == "identity"
        o_ref[...] = acc.astype(BF16)


def _linear(hn, w, kind, tm, rows_per_seq, extra=()):
    m = hn.shape[0]
    n = w.shape[1]
    assert m % tm == 0 and rows_per_seq % tm == 0 and n % PROJ_TN == 0 and tm % HALO == 0
    seq_tiles = rows_per_seq // tm
    in_specs = [
        pl.BlockSpec((tm, D_MODEL), lambda i, j: (i, 0)),
        pl.BlockSpec((D_MODEL, PROJ_TN), lambda i, j: (0, j)),
    ]
    args = [hn, w]
    if kind == "rotary":
        pos = pl.BlockSpec((tm, RET_DK // 2), lambda i, j: (i % seq_tiles, 0))
        in_specs += [pos, pos]
        args += list(extra)
    elif kind == "conv":
        meta_hn, conv_w, conv_b = extra
        in_specs += [
            pl.BlockSpec((HALO, D_MODEL), lambda i, j: (jnp.maximum(i * (tm // HALO) - 1, 0), 0)),
            pl.BlockSpec((HALO, D_MODEL), lambda i, j: (0, 0)),
            pl.BlockSpec((SSD_CONV, PROJ_TN), lambda i, j: (0, j)),
            pl.BlockSpec((1, PROJ_TN), lambda i, j: (0, j)),
        ]
        args += [hn, meta_hn, conv_w, conv_b]
    return pl.pallas_call(
        functools.partial(_lin_kernel, kind=kind, tm=tm, seq_tiles=seq_tiles),
        grid=(m // tm, n // PROJ_TN),
        in_specs=in_specs,
        out_specs=pl.BlockSpec((tm, PROJ_TN), lambda i, j: (i, j)),
        out_shape=jax.ShapeDtypeStruct((m, n), BF16),
        compiler_params=_params("parallel", "arbitrary"),
        name="proj_" + kind,
    )(*args)


def _ret_tables(t):
    log_g = np.log1p(-np.exp2(-5.0 - np.arange(RET_HEADS, dtype=np.float64)))
    idx = np.arange(t, dtype=np.float64)
    diff = idx[:, None] - idx[None, :]
    dmask = np.where(diff >= 0, np.exp(log_g[:, None, None] * np.maximum(diff, 0.0)), 0.0)
    xi = np.exp(log_g[:, None] * (idx[None, :] + 1.0))
    zeta = np.exp(log_g[:, None] * (t - 1.0 - idx[None, :]))
    g_chunk = np.exp(log_g * t)
    xi = np.broadcast_to(xi[:, :, None], (RET_HEADS, t, RET_DK))
    zeta = np.broadcast_to(zeta[:, :, None], (RET_HEADS, t, RET_DK))
    return (jnp.asarray(dmask, F32), jnp.asarray(xi, F32), jnp.asarray(zeta, F32),
            tuple(float(v) for v in g_chunk))


def _ret_kernel(*refs, g_chunk, emit_out):
    if emit_out:
        (q_ref, k_ref, v_ref, st0_ref, dmask_ref, xi_ref, zeta_ref, g_ref, w_ref,
         o_ref, st_ref, ret_ref) = refs
    else:
        (q_ref, k_ref, v_ref, st0_ref, dmask_ref, xi_ref, zeta_ref, stout_ref, st_ref) = refs
    c = pl.program_id(1)

    @pl.when(c == 0)
    def _():
        st_ref[...] = st0_ref[...]

    for h in range(RET_HEADS):
        qh = q_ref[0, :, h * RET_DK:(h + 1) * RET_DK]
        kh = k_ref[0, :, h * RET_DK:(h + 1) * RET_DK]
        vh = v_ref[0, :, h * RET_DV:(h + 1) * RET_DV]
        s = _dot_nt(qh, kh) * dmask_ref[h]
        qx = (qh.astype(F32) * xi_ref[h]).astype(BF16)
        kz = (kh.astype(F32) * zeta_ref[h]).astype(BF16)
        st = st_ref[h]
        if emit_out:
            out = _dot(s.astype(BF16), vh) + _dot(qx, st.astype(BF16))
            r = out * lax.rsqrt(jnp.mean(out * out, axis=-1, keepdims=True) + EPS)
            gate = g_ref[0, :, h * RET_DV:(h + 1) * RET_DV].astype(F32)
            ret_ref[:, h * RET_DV:(h + 1) * RET_DV] = (r * gate).astype(BF16)
        st_ref[h] = st * g_chunk[h] + _dot_tn(kz, vh)

    if emit_out:
        o_ref[0] = _dot(ret_ref[...], w_ref[...]).astype(BF16)
    else:
        @pl.when(c == pl.num_programs(1) - 1)
        def _():
            stout_ref[...] = st_ref[...]


def _retention(qk3d, v3d, gz3d, st0, w_ret, t, emit_out):
    b, l, _ = qk3d.shape
    assert l % t == 0
    nc = l // t
    dmask, xi, zeta, g_chunk = _ret_tables(t)
    const3 = lambda bb, cc: (0, 0, 0)
    in_specs = [
        pl.BlockSpec((1, t, RET_QK), lambda bb, cc: (bb, cc, 0)),
        pl.BlockSpec((1, t, RET_QK), lambda bb, cc: (bb, cc, 1)),
        pl.BlockSpec((1, t, RET_V), lambda bb, cc: (bb, cc, 0)),
        pl.BlockSpec((RET_HEADS, RET_DK, RET_DV), const3),
        pl.BlockSpec((RET_HEADS, t, t), const3),
        pl.BlockSpec((RET_HEADS, t, RET_DK), const3),
        pl.BlockSpec((RET_HEADS, t, RET_DK), const3),
    ]
    args = [qk3d, qk3d, v3d, st0, dmask, xi, zeta]
    scratch = [pltpu.VMEM((RET_HEADS, RET_DK, RET_DV), F32)]
    if emit_out:
        in_specs += [
            pl.BlockSpec((1, t, RET_V), lambda bb, cc: (bb, cc, 0)),
            pl.BlockSpec((RET_V, D_MODEL), lambda bb, cc: (0, 0)),
        ]
        args += [gz3d, w_ret]
        out_specs = pl.BlockSpec((1, t, D_MODEL), lambda bb, cc: (bb, cc, 0))
        out_shape = jax.ShapeDtypeStruct((b, l, D_MODEL), BF16)
        scratch.append(pltpu.VMEM((t, RET_V), BF16))
    else:
        assert b == 1
        out_specs = pl.BlockSpec((RET_HEADS, RET_DK, RET_DV), const3)
        out_shape = jax.ShapeDtypeStruct((RET_HEADS, RET_DK, RET_DV), F32)
    return pl.pallas_call(
        functools.partial(_ret_kernel, g_chunk=g_chunk, emit_out=emit_out),
        grid=(b, nc),
        in_specs=in_specs,
        out_specs=out_specs,
        out_shape=out_shape,
        scratch_shapes=scratch,
        compiler_params=_params("parallel", "arbitrary"),
        name="retention" if emit_out else "retention_meta",
    )(*args)


def _ssd_kernel(*refs, t, emit_out):
    if emit_out:
        (xbc_ref, dt_ref, st0_ref, dtb_ref, alog_ref, e_ref,
         z_ref, dskip_ref, norm_ref, w_ref, o_ref, st_ref, y_ref) = refs
    else:
        (xbc_ref, dt_ref, st0_ref, dtb_ref, alog_ref, e_ref, stout_ref, st_ref) = refs
    c = pl.program_id(1)

    @pl.when(c == 0)
    def _():
        st_ref[...] = st0_ref[...]

    dt_raw = dt_ref[0] + dtb_ref[...]
    dt = jnp.maximum(dt_raw, 0.0) + jnp.log1p(jnp.exp(-jnp.abs(dt_raw)))
    da = dt * (-jnp.exp(alog_ref[...]))
    row = lax.broadcasted_iota(jnp.int32, (t, t), 0)
    col = lax.broadcasted_iota(jnp.int32, (t, t), 1)
    causal = row >= col
    tril = jnp.where(causal, 1.0, 0.0).astype(BF16)
    cs = sum(_dot(tril, part) for part in _split3(da))
    ecs = jnp.exp(cs)
    edl = jnp.exp(cs[t - 1:t, :] - cs)
    cs2 = cs * LOG2E
    cs2_t = cs2.T
    lane = lax.broadcasted_iota(jnp.int32, (t, DT_W), 1)

    def expand(v):
        hi, lo = _split2(v)
        return _dot(jnp.where(lane < SSD_HEADS, hi, lo), e_ref[...])

    xs = xbc_ref[0, :, :SSD_DINNER].astype(F32)
    xdt_f = xs * expand(dt)
    xdt = xdt_f.astype(BF16)
    xdec = (xdt_f * expand(edl)).astype(BF16)
    ecs_e = expand(ecs)
    head_of_col = lax.broadcasted_iota(jnp.int32, (t, SSD_NORM_GROUP), 1) // SSD_HEADDIM
    zeros_b = jnp.zeros((t, SSD_STATE), BF16)

    def b_cols(g):
        return xbc_ref[0, :, SSD_DINNER + g * SSD_STATE:SSD_DINNER + (g + 1) * SSD_STATE]

    for pair in range(SSD_GROUPS // 2):
        if emit_out:
            c2 = xbc_ref[0, :, SSD_DINNER + SSD_BC + 2 * pair * SSD_STATE:
                         SSD_DINNER + SSD_BC + (2 * pair + 2) * SSD_STATE]
            b2 = jnp.concatenate([jnp.concatenate([b_cols(2 * pair), zeros_b], axis=1),
                                  jnp.concatenate([zeros_b, b_cols(2 * pair + 1)], axis=1)], axis=0)
            cb2 = _dot_nt(c2, b2)
        for half in range(2):
            g = 2 * pair + half
            gs = slice(g * SSD_NORM_GROUP, (g + 1) * SSD_NORM_GROUP)
            bg = b_cols(g)
            stg = st_ref[:, gs]
            if emit_out:
                cg = c2[:, half * SSD_STATE:(half + 1) * SSD_STATE]
                cb = jnp.where(causal, cb2[:, half * t:(half + 1) * t], 0.0)
                xg = xdt[:, gs]
                ms, xm = [], []
                for h in range(SSD_HPG):
                    hh = g * SSD_HPG + h
                    seg = jnp.minimum(cs2[:, hh:hh + 1] - cs2_t[hh:hh + 1, :], 0.0)
                    ms.append((cb * jnp.exp2(seg)).astype(BF16))
                    xm.append(jnp.where(head_of_col == h, xg, jnp.zeros_like(xg)))
                y = _dot(jnp.concatenate(ms, axis=1), jnp.concatenate(xm, axis=0))
                y = y + _dot(cg, stg.astype(BF16)) * ecs_e[:, gs]
                y = y + dskip_ref[:, gs] * xs[:, gs]
                y = y * z_ref[0, :, gs].astype(F32)
                y = y * lax.rsqrt(jnp.mean(y * y, axis=-1, keepdims=True) + EPS)
                y_ref[:, gs] = (y * norm_ref[:, gs]).astype(BF16)
            st_ref[:, gs] = stg * ecs_e[t - 1:t, gs] + _dot_tn(bg, xdec[:, gs])

    if emit_out:
        o_ref[0] = _dot(y_ref[...], w_ref[...]).astype(BF16)
    else:
        @pl.when(c == pl.num_programs(1) - 1)
        def _():
            stout_ref[...] = st_ref[...]


def _ssd(xbc3d, gz3d, dt3d, st0, dt_bias, a_log, expand_mat, dskip_e, ssd_norm, w_ssd, t, emit_out):
    b, l, _ = xbc3d.shape
    assert l % t == 0
    nc = l // t
    const2 = lambda bb, cc: (0, 0)
    in_specs = [
        pl.BlockSpec((1, t, SSD_XBC), lambda bb, cc: (bb, cc, 0)),
        pl.BlockSpec((1, t, DT_W), lambda bb, cc: (bb, cc, 0)),
        pl.BlockSpec((SSD_STATE, SSD_DINNER), const2),
        pl.BlockSpec((1, DT_W), const2),
        pl.BlockSpec((1, DT_W), const2),
        pl.BlockSpec((DT_W, SSD_DINNER), const2),
    ]
    args = [xbc3d, dt3d, st0, dt_bias, a_log, expand_mat]
    scratch = [pltpu.VMEM((SSD_STATE, SSD_DINNER), F32)]
    if emit_out:
        in_specs += [
            pl.BlockSpec((1, t, SSD_DINNER), lambda bb, cc: (bb, cc, RET_V // SSD_DINNER)),
            pl.BlockSpec((1, SSD_DINNER), const2),
            pl.BlockSpec((1, SSD_DINNER), const2),
            pl.BlockSpec((SSD_DINNER, D_MODEL), const2),
        ]
        args += [gz3d, dskip_e, ssd_norm, w_ssd]
        out_specs = pl.BlockSpec((1, t, D_MODEL), lambda bb, cc: (bb, cc, 0))
        out_shape = jax.ShapeDtypeStruct((b, l, D_MODEL), BF16)
        scratch.append(pltpu.VMEM((t, SSD_DINNER), BF16))
    else:
        assert b == 1
        out_specs = pl.BlockSpec((SSD_STATE, SSD_DINNER), const2)
        out_shape = jax.ShapeDtypeStruct((SSD_STATE, SSD_DINNER), F32)
    return pl.pallas_call(
        functools.partial(_ssd_kernel, t=t, emit_out=emit_out),
        grid=(b, nc),
        in_specs=in_specs,
        out_specs=out_specs,
        out_shape=out_shape,
        scratch_shapes=scratch,
        compiler_params=_params("parallel", "arbitrary"),
        name="ssd" if emit_out else "ssd_meta",
    )(*args)


def _tail_kernel(x_ref, ba_ref, bb_ref, gates_ref, wout_ref, npost_ref, nfpre_ref, wg_ref, wu_ref,
                 wd_ref, nfpost_ref, o_ref):
    ga = gates_ref[:, :D_MODEL].astype(F32)
    gb = gates_ref[:, D_MODEL:].astype(F32)
    merged = ga * ba_ref[...].astype(F32) + gb * bb_ref[...].astype(F32)
    mix = _dot(merged.astype(BF16), wout_ref[...])
    h1 = x_ref[...] + _rms(mix, npost_ref[...])
    u = _rms(h1, nfpre_ref[...]).astype(BF16)
    act = (_silu(_dot(u, wg_ref[...])) * _dot(u, wu_ref[...])).astype(BF16)
    f = _dot(act, wd_ref[...])
    o_ref[...] = h1 + _rms(f, nfpost_ref[...])


def _tail(x2d, ba, bb, gates, w_out, n_post, nf_pre, w_gate, w_up, w_down, nf_post, tm):
    m = x2d.shape[0]
    assert m % tm == 0
    row = lambda i: (i, 0)
    const = lambda i: (0, 0)
    resident = functools.partial(pl.BlockSpec, index_map=const, pipeline_mode=pl.Buffered(1))
    return pl.pallas_call(
        _tail_kernel,
        grid=(m // tm,),
        in_specs=[
            pl.BlockSpec((tm, D_MODEL), row),
            pl.BlockSpec((tm, D_MODEL), row),
            pl.BlockSpec((tm, D_MODEL), row),
            pl.BlockSpec((tm, 2 * D_MODEL), row),
            resident((D_MODEL, D_MODEL)),
            pl.BlockSpec((1, D_MODEL), const),
            pl.BlockSpec((1, D_MODEL), const),
            resident((D_MODEL, D_FF)),
            resident((D_MODEL, D_FF)),
            resident((D_FF, D_MODEL)),
            pl.BlockSpec((1, D_MODEL), const),
        ],
        out_specs=pl.BlockSpec((tm, D_MODEL), row),
        out_shape=jax.ShapeDtypeStruct((m, D_MODEL), F32),
        compiler_params=_params("parallel"),
        name="tail",
    )(x2d, ba, bb, gates, w_out, n_post, nf_pre, w_gate, w_up, w_down, nf_post)


PROJ_TM = 1024
RET_T = 256
SSD_T = 128
TAIL_TM = 512


def _input_projection(x2d, meta_hn, gain, w_dt, weights, cos, sin, conv_w, conv_b, tm):
    rows_per_seq = cos.shape[0]
    hn, dt = _prenorm(x2d, gain, w_dt, tm)
    lin = functools.partial(_linear, hn, tm=tm, rows_per_seq=rows_per_seq)
    qk = lin(weights["qk"], "rotary", extra=(cos, sin))
    v = lin(weights["v"], "identity")
    gz = lin(weights["gz"], "silu")
    xbc = lin(weights["xbc"], "conv", extra=(meta_hn, conv_w, conv_b))
    gates = lin(weights["gates"], "sigmoid")
    return hn, dt, qk, v, gz, xbc, gates


def kernel(x, meta_tokens, norm_mix_pre, w_in, conv_w, conv_b, dt_bias, a_log, d_skip, ssd_norm,
           w_ret_branch, w_ssd_branch, w_out, norm_mix_post, norm_ffn_pre, w_gate, w_up, w_down,
           norm_ffn_post):
    b, seq, _ = x.shape
    assert norm_mix_pre.shape[0] == 1, "single-layer block"
    assert meta_tokens.shape[0] == N_META == HALO

    w_in0 = w_in[0]
    weights = {
        "qk": w_in0[:, COL_Q:COL_V].astype(BF16),
        "v": w_in0[:, COL_V:COL_G].astype(BF16),
        "gz": w_in0[:, COL_G:COL_XBC].astype(BF16),
        "xbc": w_in0[:, COL_XBC:COL_DT].astype(BF16),
        "gates": w_in0[:, COL_GATES:].astype(BF16),
    }
    w_dt1 = w_in0[:, COL_DT:COL_GATES]
    w_dt = jnp.pad(jnp.concatenate([w_dt1, w_dt1], axis=1), ((0, 0), (0, 128 - DT_W))).astype(BF16)
    w_ret = w_ret_branch[0].astype(BF16)
    w_ssd = w_ssd_branch[0].astype(BF16)
    expand_mat = jnp.tile(jnp.repeat(jnp.eye(SSD_HEADS, dtype=BF16), SSD_HEADDIM, axis=1), (2, 1))
    dskip_e = jnp.repeat(d_skip[0], SSD_HEADDIM)[None, :]
    gain_pre = norm_mix_pre[0][None, :]
    conv_b2 = conv_b[0][None, :]
    dt_bias2 = jnp.tile(dt_bias[0], 2)[None, :]
    a_log2 = jnp.tile(a_log[0], 2)[None, :]
    ssd_norm2 = ssd_norm[0][None, :]

    half = RET_DK // 2
    inv = ROPE_BASE ** (-jnp.arange(half, dtype=F32) / half)
    ang = jnp.arange(N_META + seq, dtype=F32)[:, None] * inv[None, :]
    cos, sin = jnp.cos(ang), jnp.sin(ang)

    meta_hn, dt_m, qk_m, v_m, _, xbc_m, _ = _input_projection(
        meta_tokens.astype(F32), jnp.zeros((HALO, D_MODEL), BF16), gain_pre, w_dt, weights,
        cos[:N_META], sin[:N_META], conv_w[0], conv_b2, N_META)
    ret_st = _retention(qk_m[None], v_m[None], None, jnp.zeros((RET_HEADS, RET_DK, RET_DV), F32),
                        None, N_META, False)
    ssd_st = _ssd(xbc_m[None], None, dt_m[None], jnp.zeros((SSD_STATE, SSD_DINNER), F32), dt_bias2,
                  a_log2, expand_mat, None, None, None, N_META, False)

    x2d = x.reshape(b * seq, D_MODEL)
    _, dt, qk, v, gz, xbc, gates = _input_projection(
        x2d, meta_hn, gain_pre, w_dt, weights, cos[N_META:], sin[N_META:], conv_w[0], conv_b2, PROJ_TM)
    as3d = lambda a: a.reshape(b, seq, a.shape[-1])
    branch_a = _retention(as3d(qk), as3d(v), as3d(gz), ret_st, w_ret, RET_T, True)
    branch_b = _ssd(as3d(xbc), as3d(gz), as3d(dt), ssd_st, dt_bias2, a_log2, expand_mat, dskip_e,
                    ssd_norm2, w_ssd, SSD_T, True)
    out = _tail(x2d, branch_a.reshape(b * seq, D_MODEL), branch_b.reshape(b * seq, D_MODEL), gates,
                w_out[0].astype(BF16), norm_mix_post[0][None, :], norm_ffn_pre[0][None, :],
                w_gate[0].astype(BF16), w_up[0].astype(BF16), w_down[0].astype(BF16),
                norm_ffn_post[0][None, :], TAIL_TM)
    return out.reshape(b, seq, D_MODEL)
```

```python
import functools

import numpy as np
import jax
import jax.numpy as jnp
from jax import lax
from jax.experimental import pallas as pl
from jax.experimental.pallas import tpu as pltpu

F32 = jnp.float32
BF16 = jnp.bfloat16

D_MODEL = 1024
N_META = 16
EPS = 1e-6
RET_HEADS = 4
RET_DK = 256
RET_DV = 512
RET_QK = RET_HEADS * RET_DK
RET_V = RET_HEADS * RET_DV
ROPE_BASE = 10000.0
SSD_DINNER = 2 * D_MODEL
SSD_HEADDIM = 64
SSD_HEADS = SSD_DINNER // SSD_HEADDIM
SSD_GROUPS = 8
SSD_HPG = SSD_HEADS // SSD_GROUPS
SSD_STATE = 128
SSD_CONV = 4
SSD_BC = SSD_GROUPS * SSD_STATE
SSD_XBC = SSD_DINNER + 2 * SSD_BC
SSD_NORM_GROUP = SSD_DINNER // SSD_GROUPS
D_FF = 2816
LOG2E = 1.4426950408889634

COL_Q = 0
COL_K = RET_QK
COL_V = 2 * RET_QK
COL_G = COL_V + RET_V
COL_Z = COL_G + RET_V
COL_XBC = COL_Z + SSD_DINNER
COL_DT = COL_XBC + SSD_XBC
COL_GATES = COL_DT + SSD_HEADS
PROJ_TN = 1024
HALO = 16
SEG_PAD = 8
DT_W = 2 * SSD_HEADS

VMEM_LIMIT = 56 * 1024 * 1024


def _dot(a, b):
    return jnp.dot(a, b, preferred_element_type=F32)


def _dot_nt(a, b):
    return lax.dot_general(a, b, (((1,), (1,)), ((), ())), preferred_element_type=F32)


def _dot_tn(a, b):
    return lax.dot_general(a, b, (((0,), (0,)), ((), ())), preferred_element_type=F32)


def _silu(x):
    return x * jax.nn.sigmoid(x)


def _split2(v):
    hi = v.astype(BF16)
    lo = (v - hi.astype(F32)).astype(BF16)
    return hi, lo


def _split3(v):
    hi = v.astype(BF16)
    r = v - hi.astype(F32)
    mid = r.astype(BF16)
    lo = (r - mid.astype(F32)).astype(BF16)
    return hi, mid, lo


def _rms(v, gain):
    return v * lax.rsqrt(jnp.mean(v * v, axis=-1, keepdims=True) + EPS) * gain


def _params(*sem):
    return pltpu.CompilerParams(dimension_semantics=sem, vmem_limit_bytes=VMEM_LIMIT)


def _lane_blocks(ref_or_val, rows, width):
    return [ref_or_val[rows, c * 128:(c + 1) * 128] for c in range(width // 128)]


def _norm_kernel(*refs, tm, permute):
    if permute:
        x_ref, g_ref, wdt_ref, hn_ref, dt_ref, hnp_ref, stg_ref = refs
    else:
        x_ref, g_ref, wdt_ref, hn_ref, dt_ref = refs
    hn_f = _rms(x_ref[...], g_ref[...])
    hn = hn_f.astype(BF16)
    hn_ref[...] = hn
    dt_ref[...] = _dot(hn, wdt_ref[...])[:, :DT_W]
    if permute:
        seg = tm // 8
        groups = seg // 8
        for m in range(tm // 8):
            i, j0 = m // groups, 8 * (m % groups)
            for c, blk in enumerate(_lane_blocks(hn_f, slice(8 * m, 8 * m + 8), D_MODEL)):
                stg_ref[c, pl.ds(8 * j0 + i, 8, stride=8), :] = blk
        hnp_ref[...] = jnp.concatenate(
            [stg_ref[c] for c in range(D_MODEL // 128)], axis=1).astype(BF16)


def _prenorm(x2d, gain, w_dt, tm, permute):
    m = x2d.shape[0]
    assert m % tm == 0
    row = lambda i: (i, 0)
    out_specs = [pl.BlockSpec((tm, D_MODEL), row), pl.BlockSpec((tm, DT_W), row)]
    out_shape = [jax.ShapeDtypeStruct((m, D_MODEL), BF16), jax.ShapeDtypeStruct((m, DT_W), F32)]
    scratch = []
    if permute:
        assert tm % 64 == 0
        out_specs.append(pl.BlockSpec((tm, D_MODEL), row))
        out_shape.append(jax.ShapeDtypeStruct((m, D_MODEL), BF16))
        scratch.append(pltpu.VMEM((D_MODEL // 128, tm, 128), F32))
    return pl.pallas_call(
        functools.partial(_norm_kernel, tm=tm, permute=permute),
        grid=(m // tm,),
        in_specs=[
            pl.BlockSpec((tm, D_MODEL), row),
            pl.BlockSpec((1, D_MODEL), lambda i: (0, 0)),
            pl.BlockSpec((D_MODEL, 128), lambda i: (0, 0)),
        ],
        out_specs=out_specs,
        out_shape=out_shape,
        scratch_shapes=scratch,
        compiler_params=_params("parallel"),
        name="prenorm",
    )(x2d, gain, w_dt)


def _lin_kernel(*refs, kind, tm, n, seq_tiles):
    ncol = n // PROJ_TN
    cols = lambda jt: slice(jt * PROJ_TN, (jt + 1) * PROJ_TN)
    if kind == "rotary":
        hn_ref, w_ref, cos_ref, sin_ref, o_ref = refs
        half = RET_DK // 2
        for jt in range(ncol):
            scale = 1.0 if jt == 0 else RET_DK ** -0.5
            acc = _dot(hn_ref[...], w_ref[:, cols(jt)])
            cos = cos_ref[...] * scale
            sin = sin_ref[...] * scale
            for h in range(RET_HEADS):
                lo = h * RET_DK
                t1 = acc[:, lo:lo + half]
                t2 = acc[:, lo + half:lo + RET_DK]
                base = jt * PROJ_TN + lo
                o_ref[:, base:base + half] = (t1 * cos - t2 * sin).astype(BF16)
                o_ref[:, base + half:base + RET_DK] = (t1 * sin + t2 * cos).astype(BF16)
    elif kind == "conv":
        hn_ref, w_ref, halo_ref, meta_ref, cw_ref, cb_ref, o_ref = refs
        first = (pl.program_id(0) % seq_tiles) == 0
        lhs = jnp.concatenate([jnp.where(first, meta_ref[...], halo_ref[...]), hn_ref[...]], axis=0)
        rows = tm + 8
        for jt in range(ncol):
            acc = _dot(lhs, w_ref[:, cols(jt)])
            full = acc[HALO - 8:, :]
            out = cb_ref[:, cols(jt)] + cw_ref[SSD_CONV - 1:SSD_CONV, cols(jt)] * full[8:, :]
            for k in range(SSD_CONV - 1):
                back = SSD_CONV - 1 - k
                win = pltpu.roll(full, rows - (8 - back), axis=0)[:tm, :]
                out = out + cw_ref[k:k + 1, cols(jt)] * win
            o_ref[:, cols(jt)] = _silu(out).astype(BF16)
    elif kind == "conv_interleaved":
        hnp_ref, w_ref, halo_ref, meta_ref, cw_ref, cb_ref, o_ref, stg_ref = refs
        first = (pl.program_id(0) % seq_tiles) == 0
        lhs = jnp.concatenate([hnp_ref[...], jnp.where(first, meta_ref[...], halo_ref[...])], axis=0)
        seg = tm // 8
        sub = lax.broadcasted_iota(jnp.int32, (8, PROJ_TN), 0)
        for jt in range(ncol):
            acc = _dot(lhs, w_ref[:, cols(jt)])
            accp, lefts = acc[:tm, :], acc[tm:, :]
            out = cb_ref[:, cols(jt)] + cw_ref[SSD_CONV - 1:SSD_CONV, cols(jt)] * accp
            for back in range(1, SSD_CONV):
                heads = []
                for q in range(back):
                    src = accp[8 * (seg + q - back):8 * (seg + q - back) + 8, :]
                    prev = pltpu.roll(src, 1, axis=0)
                    before = lefts[HALO + q - back:HALO + q - back + 1, :]
                    heads.append(jnp.where(sub == 0, before, prev))
                shifted = jnp.concatenate(heads + [accp[:tm - 8 * back, :]], axis=0)
                out = out + cw_ref[SSD_CONV - 1 - back:SSD_CONV - back, cols(jt)] * shifted
            act = _silu(out)
            for j in range(seg):
                for c, blk in enumerate(_lane_blocks(act, slice(8 * j, 8 * j + 8), PROJ_TN)):
                    stg_ref[jt % 2, c, pl.ds(j, 8, stride=seg + SEG_PAD), :] = blk
            for i in range(8):
                rows = slice((seg + SEG_PAD) * i, (seg + SEG_PAD) * i + seg)
                o_ref[seg * i:seg * (i + 1), cols(jt)] = jnp.concatenate(
                    [stg_ref[jt % 2, c, rows, :] for c in range(PROJ_TN // 128)], axis=1).astype(BF16)
    else:
        hn_ref, w_ref, o_ref = refs
        for jt in range(ncol):
            acc = _dot(hn_ref[...], w_ref[:, cols(jt)])
            if kind == "silu":
                acc = _silu(acc)
            elif kind == "sigmoid":
                acc = jax.nn.sigmoid(acc)
            else:
                assert kind == "identity"
            o_ref[:, cols(jt)] = acc.astype(BF16)


def _linear(hn, w, kind, tm, rows_per_seq, extra=()):
    m = hn.shape[0]
    n = w.shape[1]
    assert m % tm == 0 and rows_per_seq % tm == 0 and n % PROJ_TN == 0 and tm % HALO == 0
    seq_tiles = rows_per_seq // tm
    const = lambda i: (0, 0)
    resident = functools.partial(pl.BlockSpec, index_map=const, pipeline_mode=pl.Buffered(1))
    in_specs = [pl.BlockSpec((tm, D_MODEL), lambda i: (i, 0)), resident((D_MODEL, n))]
    args = [hn, w]
    if kind == "rotary":
        pos = pl.BlockSpec((tm, RET_DK // 2), lambda i: (i % seq_tiles, 0))
        in_specs += [pos, pos]
        args += list(extra)
    elif kind in ("conv", "conv_interleaved"):
        hn_nat, meta_hn, conv_w, conv_b = extra
        in_specs += [
            pl.BlockSpec((HALO, D_MODEL), lambda i: (jnp.maximum(i * (tm // HALO) - 1, 0), 0)),
            pl.BlockSpec((HALO, D_MODEL), const),
            pl.BlockSpec((SSD_CONV, n), const),
            pl.BlockSpec((1, n), const),
        ]
        args += [hn_nat, meta_hn, conv_w, conv_b]
    scratch = []
    if kind == "conv_interleaved":
        assert tm % 64 == 0
        scratch.append(pltpu.VMEM((2, PROJ_TN // 128, 8 * (tm // 8 + SEG_PAD), 128), F32))
    return pl.pallas_call(
        functools.partial(_lin_kernel, kind=kind, tm=tm, n=n, seq_tiles=seq_tiles),
        grid=(m // tm,),
        in_specs=in_specs,
        out_specs=pl.BlockSpec((tm, n), lambda i: (i, 0)),
        out_shape=jax.ShapeDtypeStruct((m, n), BF16),
        scratch_shapes=scratch,
        compiler_params=_params("parallel"),
        name="proj_" + kind,
    )(*args)


def _ret_tables(t):
    log_g = np.log1p(-np.exp2(-5.0 - np.arange(RET_HEADS, dtype=np.float64)))
    idx = np.arange(t, dtype=np.float64)
    diff = idx[:, None] - idx[None, :]
    dmask = np.where(diff >= 0, np.exp(log_g[:, None, None] * np.maximum(diff, 0.0)), 0.0)
    xi = np.exp(log_g[:, None] * (idx[None, :] + 1.0))
    zeta = np.exp(log_g[:, None] * (t - 1.0 - idx[None, :]))
    g_chunk = np.exp(log_g * t)
    xi = np.broadcast_to(xi[:, :, None], (RET_HEADS, t, RET_DK))
    zeta = np.broadcast_to(zeta[:, :, None], (RET_HEADS, t, RET_DK))
    return (jnp.asarray(dmask, F32), jnp.asarray(xi, F32), jnp.asarray(zeta, F32),
            tuple(float(v) for v in g_chunk))


def _ret_kernel(*refs, g_chunk, emit_out):
    if emit_out:
        (q_ref, k_ref, v_ref, st0_ref, dmask_ref, xi_ref, zeta_ref, g_ref, w_ref,
         o_ref, st_ref, ret_ref) = refs
    else:
        (q_ref, k_ref, v_ref, st0_ref, dmask_ref, xi_ref, zeta_ref, stout_ref, st_ref) = refs
    c = pl.program_id(1)

    @pl.when(c == 0)
    def _():
        st_ref[...] = st0_ref[...]

    for h in range(RET_HEADS):
        qh = q_ref[0, :, h * RET_DK:(h + 1) * RET_DK]
        kh = k_ref[0, :, h * RET_DK:(h + 1) * RET_DK]
        vh = v_ref[0, :, h * RET_DV:(h + 1) * RET_DV]
        s = _dot_nt(qh, kh) * dmask_ref[h]
        qx = (qh.astype(F32) * xi_ref[h]).astype(BF16)
        kz = (kh.astype(F32) * zeta_ref[h]).astype(BF16)
        st = st_ref[h]
        if emit_out:
            out = _dot(s.astype(BF16), vh) + _dot(qx, st.astype(BF16))
            r = out * lax.rsqrt(jnp.mean(out * out, axis=-1, keepdims=True) + EPS)
            gate = g_ref[0, :, h * RET_DV:(h + 1) * RET_DV].astype(F32)
            ret_ref[:, h * RET_DV:(h + 1) * RET_DV] = (r * gate).astype(BF16)
        st_ref[h] = st * g_chunk[h] + _dot_tn(kz, vh)

    if emit_out:
        o_ref[0] = _dot(ret_ref[...], w_ref[...]).astype(BF16)
    else:
        @pl.when(c == pl.num_programs(1) - 1)
        def _():
            stout_ref[...] = st_ref[...]


def _retention(qk3d, v3d, gz3d, st0, w_ret, t, emit_out):
    b, l, _ = qk3d.shape
    assert l % t == 0
    nc = l // t
    dmask, xi, zeta, g_chunk = _ret_tables(t)
    const3 = lambda bb, cc: (0, 0, 0)
    in_specs = [
        pl.BlockSpec((1, t, RET_QK), lambda bb, cc: (bb, cc, 0)),
        pl.BlockSpec((1, t, RET_QK), lambda bb, cc: (bb, cc, 1)),
        pl.BlockSpec((1, t, RET_V), lambda bb, cc: (bb, cc, 0)),
        pl.BlockSpec((RET_HEADS, RET_DK, RET_DV), const3),
        pl.BlockSpec((RET_HEADS, t, t), const3),
        pl.BlockSpec((RET_HEADS, t, RET_DK), const3),
        pl.BlockSpec((RET_HEADS, t, RET_DK), const3),
    ]
    args = [qk3d, qk3d, v3d, st0, dmask, xi, zeta]
    scratch = [pltpu.VMEM((RET_HEADS, RET_DK, RET_DV), F32)]
    if emit_out:
        in_specs += [
            pl.BlockSpec((1, t, RET_V), lambda bb, cc: (bb, cc, 0)),
            pl.BlockSpec((RET_V, D_MODEL), lambda bb, cc: (0, 0)),
        ]
        args += [gz3d, w_ret]
        out_specs = pl.BlockSpec((1, t, D_MODEL), lambda bb, cc: (bb, cc, 0))
        out_shape = jax.ShapeDtypeStruct((b, l, D_MODEL), BF16)
        scratch.append(pltpu.VMEM((t, RET_V), BF16))
    else:
        assert b == 1
        out_specs = pl.BlockSpec((RET_HEADS, RET_DK, RET_DV), const3)
        out_shape = jax.ShapeDtypeStruct((RET_HEADS, RET_DK, RET_DV), F32)
    return pl.pallas_call(
        functools.partial(_ret_kernel, g_chunk=g_chunk, emit_out=emit_out),
        grid=(b, nc),
        in_specs=in_specs,
        out_specs=out_specs,
        out_shape=out_shape,
        scratch_shapes=scratch,
        compiler_params=_params("parallel", "arbitrary"),
        name="retention" if emit_out else "retention_meta",
    )(*args)


def _ssd_kernel(*refs, t, emit_out):
    if emit_out:
        (xbc_ref, dt_ref, st0_ref, dtb_ref, alog_ref, e_ref,
         z_ref, dskip_ref, norm_ref, w_ref, o_ref, st_ref, y_ref) = refs
    else:
        (xbc_ref, dt_ref, st0_ref, dtb_ref, alog_ref, e_ref, stout_ref, st_ref) = refs
    c = pl.program_id(1)

    @pl.when(c == 0)
    def _():
        st_ref[...] = st0_ref[...]

    dt_raw = dt_ref[0] + dtb_ref[...]
    dt = jnp.maximum(dt_raw, 0.0) + jnp.log1p(jnp.exp(-jnp.abs(dt_raw)))
    da = dt * (-jnp.exp(alog_ref[...]))
    row = lax.broadcasted_iota(jnp.int32, (t, t), 0)
    col = lax.broadcasted_iota(jnp.int32, (t, t), 1)
    causal = row >= col
    tril = jnp.where(causal, 1.0, 0.0).astype(BF16)
    cs = sum(_dot(tril, part) for part in _split3(da))
    ecs = jnp.exp(cs)
    edl = jnp.exp(cs[t - 1:t, :] - cs)
    cs2 = cs * LOG2E
    cs2_t = cs2.T
    lane = lax.broadcasted_iota(jnp.int32, (t, DT_W), 1)

    def expand(v):
        hi, lo = _split2(v)
        return _dot(jnp.where(lane < SSD_HEADS, hi, lo), e_ref[...])

    xs = xbc_ref[0, :, :SSD_DINNER].astype(F32)
    xdt_f = xs * expand(dt)
    xdt = xdt_f.astype(BF16)
    xdec = (xdt_f * expand(edl)).astype(BF16)
    ecs_e = expand(ecs)
    head_of_col = lax.broadcasted_iota(jnp.int32, (t, SSD_NORM_GROUP), 1) // SSD_HEADDIM
    zeros_b = jnp.zeros((t, SSD_STATE), BF16)

    def b_cols(g):
        return xbc_ref[0, :, SSD_DINNER + g * SSD_STATE:SSD_DINNER + (g + 1) * SSD_STATE]

    for pair in range(SSD_GROUPS // 2):
        if emit_out:
            c2 = xbc_ref[0, :, SSD_DINNER + SSD_BC + 2 * pair * SSD_STATE:
                         SSD_DINNER + SSD_BC + (2 * pair + 2) * SSD_STATE]
            b2 = jnp.concatenate([jnp.concatenate([b_cols(2 * pair), zeros_b], axis=1),
                                  jnp.concatenate([zeros_b, b_cols(2 * pair + 1)], axis=1)], axis=0)
            cb2 = _dot_nt(c2, b2)
        for half in range(2):
            g = 2 * pair + half
            gs = slice(g * SSD_NORM_GROUP, (g + 1) * SSD_NORM_GROUP)
            bg = b_cols(g)
            stg = st_ref[:, gs]
            if emit_out:
                cg = c2[:, half * SSD_STATE:(half + 1) * SSD_STATE]
                cb = jnp.where(causal, cb2[:, half * t:(half + 1) * t], 0.0)
                xg = xdt[:, gs]
                ms, xm = [], []
                for h in range(SSD_HPG):
                    hh = g * SSD_HPG + h
                    seg = jnp.minimum(cs2[:, hh:hh + 1] - cs2_t[hh:hh + 1, :], 0.0)
                    ms.append((cb * jnp.exp2(seg)).astype(BF16))
                    xm.append(jnp.where(head_of_col == h, xg, jnp.zeros_like(xg)))
                y = _dot(jnp.concatenate(ms, axis=1), jnp.concatenate(xm, axis=0))
                y = y + _dot(cg, stg.astype(BF16)) * ecs_e[:, gs]
                y = y + dskip_ref[:, gs] * xs[:, gs]
                y = y * z_ref[0, :, gs].astype(F32)
                y = y * lax.rsqrt(jnp.mean(y * y, axis=-1, keepdims=True) + EPS)
                y_ref[:, gs] = (y * norm_ref[:, gs]).astype(BF16)
            st_ref[:, gs] = stg * ecs_e[t - 1:t, gs] + _dot_tn(bg, xdec[:, gs])

    if emit_out:
        o_ref[0] = _dot(y_ref[...], w_ref[...]).astype(BF16)
    else:
        @pl.when(c == pl.num_programs(1) - 1)
        def _():
            stout_ref[...] = st_ref[...]


def _ssd(xbc3d, gz3d, dt3d, st0, dt_bias, a_log, expand_mat, dskip_e, ssd_norm, w_ssd, t, emit_out):
    b, l, _ = xbc3d.shape
    assert l % t == 0
    nc = l // t
    const2 = lambda bb, cc: (0, 0)
    in_specs = [
        pl.BlockSpec((1, t, SSD_XBC), lambda bb, cc: (bb, cc, 0)),
        pl.BlockSpec((1, t, DT_W), lambda bb, cc: (bb, cc, 0)),
        pl.BlockSpec((SSD_STATE, SSD_DINNER), const2),
        pl.BlockSpec((1, DT_W), const2),
        pl.BlockSpec((1, DT_W), const2),
        pl.BlockSpec((DT_W, SSD_DINNER), const2),
    ]
    args = [xbc3d, dt3d, st0, dt_bias, a_log, expand_mat]
    scratch = [pltpu.VMEM((SSD_STATE, SSD_DINNER), F32)]
    if emit_out:
        in_specs += [
            pl.BlockSpec((1, t, SSD_DINNER), lambda bb, cc: (bb, cc, RET_V // SSD_DINNER)),
            pl.BlockSpec((1, SSD_DINNER), const2),
            pl.BlockSpec((1, SSD_DINNER), const2),
            pl.BlockSpec((SSD_DINNER, D_MODEL), const2),
        ]
        args += [gz3d, dskip_e, ssd_norm, w_ssd]
        out_specs = pl.BlockSpec((1, t, D_MODEL), lambda bb, cc: (bb, cc, 0))
        out_shape = jax.ShapeDtypeStruct((b, l, D_MODEL), BF16)
        scratch.append(pltpu.VMEM((t, SSD_DINNER), BF16))
    else:
        assert b == 1
        out_specs = pl.BlockSpec((SSD_STATE, SSD_DINNER), const2)
        out_shape = jax.ShapeDtypeStruct((SSD_STATE, SSD_DINNER), F32)
    return pl.pallas_call(
        functools.partial(_ssd_kernel, t=t, emit_out=emit_out),
        grid=(b, nc),
        in_specs=in_specs,
        out_specs=out_specs,
        out_shape=out_shape,
        scratch_shapes=scratch,
        compiler_params=_params("parallel", "arbitrary"),
        name="ssd" if emit_out else "ssd_meta",
    )(*args)


def _tail_kernel(x_ref, ba_ref, bb_ref, gates_ref, wout_ref, npost_ref, nfpre_ref, wg_ref, wu_ref,
                 wd_ref, nfpost_ref, o_ref):
    ga = gates_ref[:, :D_MODEL].astype(F32)
    gb = gates_ref[:, D_MODEL:].astype(F32)
    merged = ga * ba_ref[...].astype(F32) + gb * bb_ref[...].astype(F32)
    mix = _dot(merged.astype(BF16), wout_ref[...])
    h1 = x_ref[...] + _rms(mix, npost_ref[...])
    u = _rms(h1, nfpre_ref[...]).astype(BF16)
    act = (_silu(_dot(u, wg_ref[...])) * _dot(u, wu_ref[...])).astype(BF16)
    f = _dot(act, wd_ref[...])
    o_ref[...] = h1 + _rms(f, nfpost_ref[...])


def _tail(x2d, ba, bb, gates, w_out, n_post, nf_pre, w_gate, w_up, w_down, nf_post, tm):
    m = x2d.shape[0]
    assert m % tm == 0
    row = lambda i: (i, 0)
    const = lambda i: (0, 0)
    resident = functools.partial(pl.BlockSpec, index_map=const, pipeline_mode=pl.Buffered(1))
    return pl.pallas_call(
        _tail_kernel,
        grid=(m // tm,),
        in_specs=[
            pl.BlockSpec((tm, D_MODEL), row),
            pl.BlockSpec((tm, D_MODEL), row),
            pl.BlockSpec((tm, D_MODEL), row),
            pl.BlockSpec((tm, 2 * D_MODEL), row),
            resident((D_MODEL, D_MODEL)),
            pl.BlockSpec((1, D_MODEL), const),
            pl.BlockSpec((1, D_MODEL), const),
            resident((D_MODEL, D_FF)),
            resident((D_MODEL, D_FF)),
            resident((D_FF, D_MODEL)),
            pl.BlockSpec((1, D_MODEL), const),
        ],
        out_specs=pl.BlockSpec((tm, D_MODEL), row),
        out_shape=jax.ShapeDtypeStruct((m, D_MODEL), F32),
        compiler_params=_params("parallel"),
        name="tail",
    )(x2d, ba, bb, gates, w_out, n_post, nf_pre, w_gate, w_up, w_down, nf_post)


PROJ_TM = 512
RET_T = 256
SSD_T = 128
TAIL_TM = 512


def _input_projection(x2d, meta_hn, gain, w_dt, weights, cos, sin, conv_w, conv_b, tm, interleave):
    rows_per_seq = cos.shape[0]
    hn, dt, *hnp = _prenorm(x2d, gain, w_dt, tm, interleave)
    lin = functools.partial(_linear, hn, tm=tm, rows_per_seq=rows_per_seq)
    qk = lin(weights["qk"], "rotary", extra=(cos, sin))
    v = lin(weights["v"], "identity")
    gz = lin(weights["gz"], "silu")
    if interleave:
        xbc = _linear(hnp[0], weights["xbc"], "conv_interleaved", tm, rows_per_seq,
                      extra=(hn, meta_hn, conv_w, conv_b))
    else:
        xbc = lin(weights["xbc"], "conv", extra=(hn, meta_hn, conv_w, conv_b))
    gates = lin(weights["gates"], "sigmoid")
    return hn, dt, qk, v, gz, xbc, gates


def kernel(x, meta_tokens, norm_mix_pre, w_in, conv_w, conv_b, dt_bias, a_log, d_skip, ssd_norm,
           w_ret_branch, w_ssd_branch, w_out, norm_mix_post, norm_ffn_pre, w_gate, w_up, w_down,
           norm_ffn_post):
    b, seq, _ = x.shape
    assert norm_mix_pre.shape[0] == 1, "single-layer block"
    assert meta_tokens.shape[0] == N_META == HALO

    w_in0 = w_in[0]
    weights = {
        "qk": w_in0[:, COL_Q:COL_V].astype(BF16),
        "v": w_in0[:, COL_V:COL_G].astype(BF16),
        "gz": w_in0[:, COL_G:COL_XBC].astype(BF16),
        "xbc": w_in0[:, COL_XBC:COL_DT].astype(BF16),
        "gates": w_in0[:, COL_GATES:].astype(BF16),
    }
    w_dt1 = w_in0[:, COL_DT:COL_GATES]
    w_dt = jnp.pad(jnp.concatenate([w_dt1, w_dt1], axis=1), ((0, 0), (0, 128 - DT_W))).astype(BF16)
    w_ret = w_ret_branch[0].astype(BF16)
    w_ssd = w_ssd_branch[0].astype(BF16)
    expand_mat = jnp.tile(jnp.repeat(jnp.eye(SSD_HEADS, dtype=BF16), SSD_HEADDIM, axis=1), (2, 1))
    dskip_e = jnp.repeat(d_skip[0], SSD_HEADDIM)[None, :]
    gain_pre = norm_mix_pre[0][None, :]
    conv_b2 = conv_b[0][None, :]
    dt_bias2 = jnp.tile(dt_bias[0], 2)[None, :]
    a_log2 = jnp.tile(a_log[0], 2)[None, :]
    ssd_norm2 = ssd_norm[0][None, :]

    half = RET_DK // 2
    inv = ROPE_BASE ** (-jnp.arange(half, dtype=F32) / half)
    ang = jnp.arange(N_META + seq, dtype=F32)[:, None] * inv[None, :]
    cos, sin = jnp.cos(ang), jnp.sin(ang)

    meta_hn, dt_m, qk_m, v_m, _, xbc_m, _ = _input_projection(
        meta_tokens.astype(F32), jnp.zeros((HALO, D_MODEL), BF16), gain_pre, w_dt, weights,
        cos[:N_META], sin[:N_META], conv_w[0], conv_b2, N_META, False)
    ret_st = _retention(qk_m[None], v_m[None], None, jnp.zeros((RET_HEADS, RET_DK, RET_DV), F32),
                        None, N_META, False)
    ssd_st = _ssd(xbc_m[None], None, dt_m[None], jnp.zeros((SSD_STATE, SSD_DINNER), F32), dt_bias2,
                  a_log2, expand_mat, None, None, None, N_META, False)

    x2d = x.reshape(b * seq, D_MODEL)
    _, dt, qk, v, gz, xbc, gates = _input_projection(
        x2d, meta_hn, gain_pre, w_dt, weights, cos[N_META:], sin[N_META:], conv_w[0], conv_b2, PROJ_TM,
        True)
    as3d = lambda a: a.reshape(b, seq, a.shape[-1])
    branch_a = _retention(as3d(qk), as3d(v), as3d(gz), ret_st, w_ret, RET_T, True)
    branch_b = _ssd(as3d(xbc), as3d(gz), as3d(dt), ssd_st, dt_bias2, a_log2, expand_mat, dskip_e,
                    ssd_norm2, w_ssd, SSD_T, True)
    out = _tail(x2d, branch_a.reshape(b * seq, D_MODEL), branch_b.reshape(b * seq, D_MODEL), gates,
                w_out[0].astype(BF16), norm_mix_post[0][None, :], norm_ffn_pre[0][None, :],
                w_gate[0].astype(BF16), w_up[0].astype(BF16), w_down[0].astype(BF16),
                norm_ffn_post[0][None, :], TAIL_TM)
    return out.reshape(b, seq, D_MODEL)
```

```python
import functools

import numpy as np
import jax
import jax.numpy as jnp
from jax import lax
from jax.experimental import pallas as pl
from jax.experimental.pallas import tpu as pltpu

F32 = jnp.float32
BF16 = jnp.bfloat16

D_MODEL = 1024
N_META = 16
EPS = 1e-6
RET_HEADS = 4
RET_DK = 256
RET_DV = 512
RET_QK = RET_HEADS * RET_DK
RET_V = RET_HEADS * RET_DV
ROPE_BASE = 10000.0
SSD_DINNER = 2 * D_MODEL
SSD_HEADDIM = 64
SSD_HEADS = SSD_DINNER // SSD_HEADDIM
SSD_GROUPS = 8
SSD_HPG = SSD_HEADS // SSD_GROUPS
SSD_STATE = 128
SSD_CONV = 4
SSD_BC = SSD_GROUPS * SSD_STATE
SSD_XBC = SSD_DINNER + 2 * SSD_BC
SSD_NORM_GROUP = SSD_DINNER // SSD_GROUPS
D_FF = 2816
LOG2E = 1.4426950408889634

COL_Q = 0
COL_K = RET_QK
COL_V = 2 * RET_QK
COL_G = COL_V + RET_V
COL_Z = COL_G + RET_V
COL_XBC = COL_Z + SSD_DINNER
COL_DT = COL_XBC + SSD_XBC
COL_GATES = COL_DT + SSD_HEADS
PROJ_TN = 1024
HALO = 16
LEAD = 32
SEG_PAD = 8
DT_W = 2 * SSD_HEADS

VMEM_LIMIT = 56 * 1024 * 1024


def _dot(a, b):
    return jnp.dot(a, b, preferred_element_type=F32)


def _dot_nt(a, b):
    return lax.dot_general(a, b, (((1,), (1,)), ((), ())), preferred_element_type=F32)


def _dot_tn(a, b):
    return lax.dot_general(a, b, (((0,), (0,)), ((), ())), preferred_element_type=F32)


def _sigmoid(x):
    return 0.5 * jnp.tanh(0.5 * x) + 0.5


def _silu(x):
    h = 0.5 * x
    return h * jnp.tanh(h) + h


def _split2(v):
    hi = v.astype(BF16)
    lo = (v - hi.astype(F32)).astype(BF16)
    return hi, lo


def _split3(v):
    hi = v.astype(BF16)
    r = v - hi.astype(F32)
    mid = r.astype(BF16)
    lo = (r - mid.astype(F32)).astype(BF16)
    return hi, mid, lo


def _rms(v, gain):
    return v * lax.rsqrt(jnp.mean(v * v, axis=-1, keepdims=True) + EPS) * gain


def _params(*sem):
    return pltpu.CompilerParams(dimension_semantics=sem, vmem_limit_bytes=VMEM_LIMIT)


def _lane_blocks(ref_or_val, rows, width):
    return [ref_or_val[rows, c * 128:(c + 1) * 128] for c in range(width // 128)]


def _norm_kernel(*refs, tm, seq_tiles, permute):
    if permute:
        x_ref, g_ref, wdt_ref, prev_ref, meta_ref, hn_ref, dt_ref, hnp_ref, stg_ref = refs
    else:
        x_ref, g_ref, wdt_ref, hn_ref, dt_ref = refs
    hn_f = _rms(x_ref[...], g_ref[...])
    hn = hn_f.astype(BF16)
    hn_ref[...] = hn
    dt_ref[...] = _dot(hn, wdt_ref[...])[:, :DT_W]
    if permute:
        seg = tm // 8
        groups = seg // 8
        back = LEAD // 8
        first = (pl.program_id(0) % seq_tiles) == 0
        left_f = _rms(jnp.where(first, meta_ref[N_META - 8:, :], prev_ref[...]), g_ref[...])
        for i in range(8):
            src = left_f if i == 0 else hn_f[i * seg - 8:i * seg, :]
            for c, blk in enumerate(_lane_blocks(src, slice(8 - back, 8), D_MODEL)):
                stg_ref[c, pl.ds(i, back, stride=8), :] = blk
        for m in range(tm // 8):
            i, j0 = m // groups, 8 * (m % groups)
            for c, blk in enumerate(_lane_blocks(hn_f, slice(8 * m, 8 * m + 8), D_MODEL)):
                stg_ref[c, pl.ds(LEAD + 8 * j0 + i, 8, stride=8), :] = blk
        hnp_ref[...] = jnp.concatenate(
            [stg_ref[c] for c in range(D_MODEL // 128)], axis=1).astype(BF16)


def _prenorm(x2d, gain, w_dt, tm, rows_per_seq, meta=None):
    m = x2d.shape[0]
    permute = meta is not None
    assert m % tm == 0 and rows_per_seq % tm == 0
    row = lambda i: (i, 0)
    const = lambda i: (0, 0)
    in_specs = [pl.BlockSpec((tm, D_MODEL), row), pl.BlockSpec((1, D_MODEL), const),
                pl.BlockSpec((D_MODEL, 128), const)]
    args = [x2d, gain, w_dt]
    out_specs = [pl.BlockSpec((tm, D_MODEL), row), pl.BlockSpec((tm, DT_W), row)]
    out_shape = [jax.ShapeDtypeStruct((m, D_MODEL), BF16), jax.ShapeDtypeStruct((m, DT_W), F32)]
    scratch = []
    if permute:
        assert tm % 64 == 0
        in_specs += [pl.BlockSpec((8, D_MODEL), lambda i: (jnp.maximum(i * (tm // 8) - 1, 0), 0)),
                     pl.BlockSpec((N_META, D_MODEL), const)]
        args += [x2d, meta]
        out_specs.append(pl.BlockSpec((tm + LEAD, D_MODEL), row))
        out_shape.append(jax.ShapeDtypeStruct((m // tm * (tm + LEAD), D_MODEL), BF16))
        scratch.append(pltpu.VMEM((D_MODEL // 128, tm + LEAD, 128), F32))
    return pl.pallas_call(
        functools.partial(_norm_kernel, tm=tm, seq_tiles=rows_per_seq // tm, permute=permute),
        grid=(m // tm,),
        in_specs=in_specs,
        out_specs=out_specs,
        out_shape=out_shape,
        scratch_shapes=scratch,
        compiler_params=_params("parallel"),
        name="prenorm",
    )(*args)


def _lin_kernel(*refs, kind, tm, n, seq_tiles):
    ncol = n // PROJ_TN
    cols = lambda jt: slice(jt * PROJ_TN, (jt + 1) * PROJ_TN)
    if kind == "rotary":
        hn_ref, w_ref, cos_ref, sin_ref, o_ref = refs
        half = RET_DK // 2
        for jt in range(ncol):
            scale = 1.0 if jt == 0 else RET_DK ** -0.5
            acc = _dot(hn_ref[...], w_ref[:, cols(jt)])
            cos = cos_ref[...] * scale
            sin = sin_ref[...] * scale
            for h in range(RET_HEADS):
                lo = h * RET_DK
                t1 = acc[:, lo:lo + half]
                t2 = acc[:, lo + half:lo + RET_DK]
                base = jt * PROJ_TN + lo
                o_ref[:, base:base + half] = (t1 * cos - t2 * sin).astype(BF16)
                o_ref[:, base + half:base + RET_DK] = (t1 * sin + t2 * cos).astype(BF16)
    elif kind == "conv":
        hn_ref, w_ref, halo_ref, meta_ref, cw_ref, cb_ref, o_ref = refs
        first = (pl.program_id(0) % seq_tiles) == 0
        lhs = jnp.concatenate([jnp.where(first, meta_ref[...], halo_ref[...]), hn_ref[...]], axis=0)
        rows = tm + 8
        for jt in range(ncol):
            acc = _dot(lhs, w_ref[:, cols(jt)])
            full = acc[HALO - 8:, :]
            out = cb_ref[:, cols(jt)] + cw_ref[SSD_CONV - 1:SSD_CONV, cols(jt)] * full[8:, :]
            for k in range(SSD_CONV - 1):
                back = SSD_CONV - 1 - k
                win = pltpu.roll(full, rows - (8 - back), axis=0)[:tm, :]
                out = out + cw_ref[k:k + 1, cols(jt)] * win
            o_ref[:, cols(jt)] = _silu(out).astype(BF16)
    elif kind == "conv_interleaved":
        hnp_ref, w_ref, cw_ref, cb_ref, o_ref, stg_ref, acc_ref = refs
        seg = tm // 8
        dyn0 = jnp.minimum(pl.program_id(0), 0)
        for jt in range(ncol):
            slot = jt % 2 + dyn0
            acc_ref[slot] = _dot(hnp_ref[...], w_ref[:, cols(jt)])
            out = cb_ref[:, cols(jt)]
            for back in range(SSD_CONV):
                k = SSD_CONV - 1 - back
                out = out + cw_ref[k:k + 1, cols(jt)] * acc_ref[slot, LEAD - 8 * back:LEAD - 8 * back + tm, :]
            act = _silu(out)
            for j in range(seg):
                for c, blk in enumerate(_lane_blocks(act, slice(8 * j, 8 * j + 8), PROJ_TN)):
                    stg_ref[jt % 2, c, pl.ds(j, 8, stride=seg + SEG_PAD), :] = blk
            for i in range(8):
                rows = slice((seg + SEG_PAD) * i, (seg + SEG_PAD) * i + seg)
                o_ref[seg * i:seg * (i + 1), cols(jt)] = jnp.concatenate(
                    [stg_ref[jt % 2, c, rows, :] for c in range(PROJ_TN // 128)], axis=1).astype(BF16)
    else:
        hn_ref, w_ref, o_ref = refs
        for jt in range(ncol):
            acc = _dot(hn_ref[...], w_ref[:, cols(jt)])
            if kind == "silu":
                acc = _silu(acc)
            elif kind == "sigmoid":
                acc = _sigmoid(acc)
            else:
                assert kind == "identity"
            o_ref[:, cols(jt)] = acc.astype(BF16)


def _linear(hn, w, kind, tm, rows_per_seq, extra=()):
    interleaved = kind == "conv_interleaved"
    rows_in = tm + LEAD if interleaved else tm
    assert hn.shape[0] % rows_in == 0
    m = hn.shape[0] // rows_in * tm
    n = w.shape[1]
    assert rows_per_seq % tm == 0 and n % PROJ_TN == 0 and tm % HALO == 0
    seq_tiles = rows_per_seq // tm
    const = lambda i: (0, 0)
    resident = functools.partial(pl.BlockSpec, index_map=const, pipeline_mode=pl.Buffered(1))
    in_specs = [pl.BlockSpec((rows_in, D_MODEL), lambda i: (i, 0)), resident((D_MODEL, n))]
    args = [hn, w]
    if kind == "rotary":
        pos = pl.BlockSpec((tm, RET_DK // 2), lambda i: (i % seq_tiles, 0))
        in_specs += [pos, pos]
        args += list(extra)
    elif kind == "conv":
        meta_hn, conv_w, conv_b = extra
        in_specs += [
            pl.BlockSpec((HALO, D_MODEL), lambda i: (jnp.maximum(i * (tm // HALO) - 1, 0), 0)),
            pl.BlockSpec((HALO, D_MODEL), const),
            pl.BlockSpec((SSD_CONV, n), const),
            pl.BlockSpec((1, n), const),
        ]
        args += [hn, meta_hn, conv_w, conv_b]
    elif interleaved:
        in_specs += [pl.BlockSpec((SSD_CONV, n), const), pl.BlockSpec((1, n), const)]
        args += list(extra)
    scratch = []
    if interleaved:
        assert tm % 64 == 0
        scratch.append(pltpu.VMEM((2, PROJ_TN // 128, 8 * (tm // 8 + SEG_PAD), 128), F32))
        scratch.append(pltpu.VMEM((2, tm + LEAD, PROJ_TN), F32))
    return pl.pallas_call(
        functools.partial(_lin_kernel, kind=kind, tm=tm, n=n, seq_tiles=seq_tiles),
        grid=(m // tm,),
        in_specs=in_specs,
        out_specs=pl.BlockSpec((tm, n), lambda i: (i, 0)),
        out_shape=jax.ShapeDtypeStruct((m, n), BF16),
        scratch_shapes=scratch,
        compiler_params=_params("parallel"),
        name="proj_" + kind,
    )(*args)


def _ret_tables(t):
    log_g = np.log1p(-np.exp2(-5.0 - np.arange(RET_HEADS, dtype=np.float64)))
    idx = np.arange(t, dtype=np.float64)
    diff = idx[:, None] - idx[None, :]
    dmask = np.where(diff >= 0, np.exp(log_g[:, None, None] * np.maximum(diff, 0.0)), 0.0)
    xi = np.exp(log_g[:, None] * (idx[None, :] + 1.0))
    zeta = np.exp(log_g[:, None] * (t - 1.0 - idx[None, :]))
    g_chunk = np.exp(log_g * t)
    xi = np.broadcast_to(xi[:, :, None], (RET_HEADS, t, RET_DK))
    zeta = np.broadcast_to(zeta[:, :, None], (RET_HEADS, t, RET_DK))
    return (jnp.asarray(dmask, F32), jnp.asarray(xi, F32), jnp.asarray(zeta, F32),
            tuple(float(v) for v in g_chunk))


def _ret_kernel(*refs, g_chunk, emit_out):
    if emit_out:
        (q_ref, k_ref, v_ref, st0_ref, dmask_ref, xi_ref, zeta_ref, g_ref, w_ref,
         o_ref, st_ref, ret_ref) = refs
    else:
        (q_ref, k_ref, v_ref, st0_ref, dmask_ref, xi_ref, zeta_ref, stout_ref, st_ref) = refs
    c = pl.program_id(1)

    @pl.when(c == 0)
    def _():
        st_ref[...] = st0_ref[...]

    for h in range(RET_HEADS):
        qh = q_ref[0, :, h * RET_DK:(h + 1) * RET_DK]
        kh = k_ref[0, :, h * RET_DK:(h + 1) * RET_DK]
        vh = v_ref[0, :, h * RET_DV:(h + 1) * RET_DV]
        s = _dot_nt(qh, kh) * dmask_ref[h]
        qx = (qh.astype(F32) * xi_ref[h]).astype(BF16)
        kz = (kh.astype(F32) * zeta_ref[h]).astype(BF16)
        st = st_ref[h]
        if emit_out:
            out = _dot(s.astype(BF16), vh) + _dot(qx, st.astype(BF16))
            r = out * lax.rsqrt(jnp.mean(out * out, axis=-1, keepdims=True) + EPS)
            gate = g_ref[0, :, h * RET_DV:(h + 1) * RET_DV].astype(F32)
            ret_ref[:, h * RET_DV:(h + 1) * RET_DV] = (r * gate).astype(BF16)
        st_ref[h] = st * g_chunk[h] + _dot_tn(kz, vh)

    if emit_out:
        o_ref[0] = _dot(ret_ref[...], w_ref[...]).astype(BF16)
    else:
        @pl.when(c == pl.num_programs(1) - 1)
        def _():
            stout_ref[...] = st_ref[...]


def _retention(qk3d, v3d, gz3d, st0, w_ret, t, emit_out):
    b, l, _ = qk3d.shape
    assert l % t == 0
    nc = l // t
    dmask, xi, zeta, g_chunk = _ret_tables(t)
    const3 = lambda bb, cc: (0, 0, 0)
    in_specs = [
        pl.BlockSpec((1, t, RET_QK), lambda bb, cc: (bb, cc, 0)),
        pl.BlockSpec((1, t, RET_QK), lambda bb, cc: (bb, cc, 1)),
        pl.BlockSpec((1, t, RET_V), lambda bb, cc: (bb, cc, 0)),
        pl.BlockSpec((RET_HEADS, RET_DK, RET_DV), const3),
        pl.BlockSpec((RET_HEADS, t, t), const3),
        pl.BlockSpec((RET_HEADS, t, RET_DK), const3),
        pl.BlockSpec((RET_HEADS, t, RET_DK), const3),
    ]
    args = [qk3d, qk3d, v3d, st0, dmask, xi, zeta]
    scratch = [pltpu.VMEM((RET_HEADS, RET_DK, RET_DV), F32)]
    if emit_out:
        in_specs += [
            pl.BlockSpec((1, t, RET_V), lambda bb, cc: (bb, cc, 0)),
            pl.BlockSpec((RET_V, D_MODEL), lambda bb, cc: (0, 0)),
        ]
        args += [gz3d, w_ret]
        out_specs = pl.BlockSpec((1, t, D_MODEL), lambda bb, cc: (bb, cc, 0))
        out_shape = jax.ShapeDtypeStruct((b, l, D_MODEL), BF16)
        scratch.append(pltpu.VMEM((t, RET_V), BF16))
    else:
        assert b == 1
        out_specs = pl.BlockSpec((RET_HEADS, RET_DK, RET_DV), const3)
        out_shape = jax.ShapeDtypeStruct((RET_HEADS, RET_DK, RET_DV), F32)
    return pl.pallas_call(
        functools.partial(_ret_kernel, g_chunk=g_chunk, emit_out=emit_out),
        grid=(b, nc),
        in_specs=in_specs,
        out_specs=out_specs,
        out_shape=out_shape,
        scratch_shapes=scratch,
        compiler_params=_params("parallel", "arbitrary"),
        name="retention" if emit_out else "retention_meta",
    )(*args)


def _ssd_kernel(*refs, t, emit_out):
    if emit_out:
        (xbc_ref, dt_ref, st0_ref, dtb_ref, alog_ref, e_ref,
         z_ref, dskip_ref, norm_ref, w_ref, o_ref, st_ref, y_ref) = refs
    else:
        (xbc_ref, dt_ref, st0_ref, dtb_ref, alog_ref, e_ref, stout_ref, st_ref) = refs
    c = pl.program_id(1)

    @pl.when(c == 0)
    def _():
        st_ref[...] = st0_ref[...]

    dt_raw = dt_ref[0] + dtb_ref[...]
    dt = jnp.maximum(dt_raw, 0.0) + jnp.log1p(jnp.exp(-jnp.abs(dt_raw)))
    da = dt * (-jnp.exp(alog_ref[...]))
    row = lax.broadcasted_iota(jnp.int32, (t, t), 0)
    col = lax.broadcasted_iota(jnp.int32, (t, t), 1)
    causal = row >= col
    tril = jnp.where(causal, 1.0, 0.0).astype(BF16)
    cs = sum(_dot(tril, part) for part in _split3(da))
    ecs = jnp.exp(cs)
    edl = jnp.exp(cs[t - 1:t, :] - cs)
    cs2 = cs * LOG2E
    cs2_t = cs2.T
    lane = lax.broadcasted_iota(jnp.int32, (t, DT_W), 1)

    def expand(v):
        hi, lo = _split2(v)
        return _dot(jnp.where(lane < SSD_HEADS, hi, lo), e_ref[...])

    xs = xbc_ref[0, :, :SSD_DINNER].astype(F32)
    xdt_f = xs * expand(dt)
    xdt = xdt_f.astype(BF16)
    xdec = (xdt_f * expand(edl)).astype(BF16)
    ecs_e = expand(ecs)
    head_of_col = lax.broadcasted_iota(jnp.int32, (t, SSD_NORM_GROUP), 1) // SSD_HEADDIM
    zeros_b = jnp.zeros((t, SSD_STATE), BF16)

    def b_cols(g):
        return xbc_ref[0, :, SSD_DINNER + g * SSD_STATE:SSD_DINNER + (g + 1) * SSD_STATE]

    for pair in range(SSD_GROUPS // 2):
        if emit_out:
            c2 = xbc_ref[0, :, SSD_DINNER + SSD_BC + 2 * pair * SSD_STATE:
                         SSD_DINNER + SSD_BC + (2 * pair + 2) * SSD_STATE]
            b2 = jnp.concatenate([jnp.concatenate([b_cols(2 * pair), zeros_b], axis=1),
                                  jnp.concatenate([zeros_b, b_cols(2 * pair + 1)], axis=1)], axis=0)
            cb2 = _dot_nt(c2, b2)
        for half in range(2):
            g = 2 * pair + half
            gs = slice(g * SSD_NORM_GROUP, (g + 1) * SSD_NORM_GROUP)
            bg = b_cols(g)
            stg = st_ref[:, gs]
            if emit_out:
                cg = c2[:, half * SSD_STATE:(half + 1) * SSD_STATE]
                cb = jnp.where(causal, cb2[:, half * t:(half + 1) * t], 0.0)
                xg = xdt[:, gs]
                ms, xm = [], []
                for h in range(SSD_HPG):
                    hh = g * SSD_HPG + h
                    seg = jnp.minimum(cs2[:, hh:hh + 1] - cs2_t[hh:hh + 1, :], 0.0)
                    ms.append((cb * jnp.exp2(seg)).astype(BF16))
                    xm.append(jnp.where(head_of_col == h, xg, jnp.zeros_like(xg)))
                y = _dot(jnp.concatenate(ms, axis=1), jnp.concatenate(xm, axis=0))
                y = y + _dot(cg, stg.astype(BF16)) * ecs_e[:, gs]
                y = y + dskip_ref[:, gs] * xs[:, gs]
                y = y * z_ref[0, :, gs].astype(F32)
                y = y * lax.rsqrt(jnp.mean(y * y, axis=-1, keepdims=True) + EPS)
                y_ref[:, gs] = (y * norm_ref[:, gs]).astype(BF16)
            st_ref[:, gs] = stg * ecs_e[t - 1:t, gs] + _dot_tn(bg, xdec[:, gs])

    if emit_out:
        o_ref[0] = _dot(y_ref[...], w_ref[...]).astype(BF16)
    else:
        @pl.when(c == pl.num_programs(1) - 1)
        def _():
            stout_ref[...] = st_ref[...]


def _ssd(xbc3d, gz3d, dt3d, st0, dt_bias, a_log, expand_mat, dskip_e, ssd_norm, w_ssd, t, emit_out):
    b, l, _ = xbc3d.shape
    assert l % t == 0
    nc = l // t
    const2 = lambda bb, cc: (0, 0)
    in_specs = [
        pl.BlockSpec((1, t, SSD_XBC), lambda bb, cc: (bb, cc, 0)),
        pl.BlockSpec((1, t, DT_W), lambda bb, cc: (bb, cc, 0)),
        pl.BlockSpec((SSD_STATE, SSD_DINNER), const2),
        pl.BlockSpec((1, DT_W), const2),
        pl.BlockSpec((1, DT_W), const2),
        pl.BlockSpec((DT_W, SSD_DINNER), const2),
    ]
    args = [xbc3d, dt3d, st0, dt_bias, a_log, expand_mat]
    scratch = [pltpu.VMEM((SSD_STATE, SSD_DINNER), F32)]
    if emit_out:
        in_specs += [
            pl.BlockSpec((1, t, SSD_DINNER), lambda bb, cc: (bb, cc, RET_V // SSD_DINNER)),
            pl.BlockSpec((1, SSD_DINNER), const2),
            pl.BlockSpec((1, SSD_DINNER), const2),
            pl.BlockSpec((SSD_DINNER, D_MODEL), const2),
        ]
        args += [gz3d, dskip_e, ssd_norm, w_ssd]
        out_specs = pl.BlockSpec((1, t, D_MODEL), lambda bb, cc: (bb, cc, 0))
        out_shape = jax.ShapeDtypeStruct((b, l, D_MODEL), BF16)
        scratch.append(pltpu.VMEM((t, SSD_DINNER), BF16))
    else:
        assert b == 1
        out_specs = pl.BlockSpec((SSD_STATE, SSD_DINNER), const2)
        out_shape = jax.ShapeDtypeStruct((SSD_STATE, SSD_DINNER), F32)
    return pl.pallas_call(
        functools.partial(_ssd_kernel, t=t, emit_out=emit_out),
        grid=(b, nc),
        in_specs=in_specs,
        out_specs=out_specs,
        out_shape=out_shape,
        scratch_shapes=scratch,
        compiler_params=_params("parallel", "arbitrary"),
        name="ssd" if emit_out else "ssd_meta",
    )(*args)


def _tail_kernel(x_ref, ba_ref, bb_ref, gates_ref, wout_ref, npost_ref, nfpre_ref, wg_ref, wu_ref,
                 wd_ref, nfpost_ref, o_ref):
    ga = gates_ref[:, :D_MODEL].astype(F32)
    gb = gates_ref[:, D_MODEL:].astype(F32)
    merged = ga * ba_ref[...].astype(F32) + gb * bb_ref[...].astype(F32)
    mix = _dot(merged.astype(BF16), wout_ref[...])
    h1 = x_ref[...] + _rms(mix, npost_ref[...])
    u = _rms(h1, nfpre_ref[...]).astype(BF16)
    act = (_silu(_dot(u, wg_ref[...])) * _dot(u, wu_ref[...])).astype(BF16)
    f = _dot(act, wd_ref[...])
    o_ref[...] = h1 + _rms(f, nfpost_ref[...])


def _tail(x2d, ba, bb, gates, w_out, n_post, nf_pre, w_gate, w_up, w_down, nf_post, tm):
    m = x2d.shape[0]
    assert m % tm == 0
    row = lambda i: (i, 0)
    const = lambda i: (0, 0)
    resident = functools.partial(pl.BlockSpec, index_map=const, pipeline_mode=pl.Buffered(1))
    return pl.pallas_call(
        _tail_kernel,
        grid=(m // tm,),
        in_specs=[
            pl.BlockSpec((tm, D_MODEL), row),
            pl.BlockSpec((tm, D_MODEL), row),
            pl.BlockSpec((tm, D_MODEL), row),
            pl.BlockSpec((tm, 2 * D_MODEL), row),
            resident((D_MODEL, D_MODEL)),
            pl.BlockSpec((1, D_MODEL), const),
            pl.BlockSpec((1, D_MODEL), const),
            resident((D_MODEL, D_FF)),
            resident((D_MODEL, D_FF)),
            resident((D_FF, D_MODEL)),
            pl.BlockSpec((1, D_MODEL), const),
        ],
        out_specs=pl.BlockSpec((tm, D_MODEL), row),
        out_shape=jax.ShapeDtypeStruct((m, D_MODEL), F32),
        compiler_params=_params("parallel"),
        name="tail",
    )(x2d, ba, bb, gates, w_out, n_post, nf_pre, w_gate, w_up, w_down, nf_post)


PROJ_TM = 512
RET_T = 256
SSD_T = 128
TAIL_TM = 512


def _input_projection(x2d, meta, gain, w_dt, weights, cos, sin, conv_w, conv_b, tm):
    rows_per_seq = cos.shape[0]
    interleave = meta is not None
    hn, dt, *hnp = _prenorm(x2d, gain, w_dt, tm, rows_per_seq, meta)
    lin = functools.partial(_linear, hn, tm=tm, rows_per_seq=rows_per_seq)
    qk = lin(weights["qk"], "rotary", extra=(cos, sin))
    v = lin(weights["v"], "identity")
    gz = lin(weights["gz"], "silu")
    if interleave:
        xbc = _linear(hnp[0], weights["xbc"], "conv_interleaved", tm, rows_per_seq,
                      extra=(conv_w, conv_b))
    else:
        xbc = lin(weights["xbc"], "conv", extra=(jnp.zeros((HALO, D_MODEL), BF16), conv_w, conv_b))
    gates = lin(weights["gates"], "sigmoid")
    return dt, qk, v, gz, xbc, gates


def kernel(x, meta_tokens, norm_mix_pre, w_in, conv_w, conv_b, dt_bias, a_log, d_skip, ssd_norm,
           w_ret_branch, w_ssd_branch, w_out, norm_mix_post, norm_ffn_pre, w_gate, w_up, w_down,
           norm_ffn_post):
    b, seq, _ = x.shape
    assert norm_mix_pre.shape[0] == 1, "single-layer block"
    assert meta_tokens.shape[0] == N_META == HALO

    w_in0 = w_in[0]
    weights = {
        "qk": w_in0[:, COL_Q:COL_V].astype(BF16),
        "v": w_in0[:, COL_V:COL_G].astype(BF16),
        "gz": w_in0[:, COL_G:COL_XBC].astype(BF16),
        "xbc": w_in0[:, COL_XBC:COL_DT].astype(BF16),
        "gates": w_in0[:, COL_GATES:].astype(BF16),
    }
    w_dt1 = w_in0[:, COL_DT:COL_GATES]
    w_dt = jnp.pad(jnp.concatenate([w_dt1, w_dt1], axis=1), ((0, 0), (0, 128 - DT_W))).astype(BF16)
    w_ret = w_ret_branch[0].astype(BF16)
    w_ssd = w_ssd_branch[0].astype(BF16)
    expand_mat = jnp.tile(jnp.repeat(jnp.eye(SSD_HEADS, dtype=BF16), SSD_HEADDIM, axis=1), (2, 1))
    dskip_e = jnp.repeat(d_skip[0], SSD_HEADDIM)[None, :]
    gain_pre = norm_mix_pre[0][None, :]
    conv_b2 = conv_b[0][None, :]
    dt_bias2 = jnp.tile(dt_bias[0], 2)[None, :]
    a_log2 = jnp.tile(a_log[0], 2)[None, :]
    ssd_norm2 = ssd_norm[0][None, :]

    half = RET_DK // 2
    inv = ROPE_BASE ** (-jnp.arange(half, dtype=F32) / half)
    ang = jnp.arange(N_META + seq, dtype=F32)[:, None] * inv[None, :]
    cos, sin = jnp.cos(ang), jnp.sin(ang)

    meta = meta_tokens.astype(F32)
    dt_m, qk_m, v_m, _, xbc_m, _ = _input_projection(
        meta, None, gain_pre, w_dt, weights, cos[:N_META], sin[:N_META], conv_w[0], conv_b2, N_META)
    ret_st = _retention(qk_m[None], v_m[None], None, jnp.zeros((RET_HEADS, RET_DK, RET_DV), F32),
                        None, N_META, False)
    ssd_st = _ssd(xbc_m[None], None, dt_m[None], jnp.zeros((SSD_STATE, SSD_DINNER), F32), dt_bias2,
                  a_log2, expand_mat, None, None, None, N_META, False)

    x2d = x.reshape(b * seq, D_MODEL)
    dt, qk, v, gz, xbc, gates = _input_projection(
        x2d, meta, gain_pre, w_dt, weights, cos[N_META:], sin[N_META:], conv_w[0], conv_b2, PROJ_TM)
    as3d = lambda a: a.reshape(b, seq, a.shape[-1])
    branch_a = _retention(as3d(qk), as3d(v), as3d(gz), ret_st, w_ret, RET_T, True)
    branch_b = _ssd(as3d(xbc), as3d(gz), as3d(dt), ssd_st, dt_bias2, a_log2, expand_mat, dskip_e,
                    ssd_norm2, w_ssd, SSD_T, True)
    out = _tail(x2d, branch_a.reshape(b * seq, D_MODEL), branch_b.reshape(b * seq, D_MODEL), gates,
                w_out[0].astype(BF16), norm_mix_post[0][None, :], norm_ffn_pre[0][None, :],
                w_gate[0].astype(BF16), w_up[0].astype(BF16), w_down[0].astype(BF16),
                norm_ffn_post[0][None, :], TAIL_TM)
    return out.reshape(b, seq, D_MODEL)
```

```python
import functools

import numpy as np
import jax
import jax.numpy as jnp
from jax import lax
from jax.experimental import pallas as pl
from jax.experimental.pallas import tpu as pltpu

F32 = jnp.float32
BF16 = jnp.bfloat16

D_MODEL = 1024
N_META = 16
EPS = 1e-6
RET_HEADS = 4
RET_DK = 256
RET_DV = 512
RET_QK = RET_HEADS * RET_DK
RET_V = RET_HEADS * RET_DV
ROPE_BASE = 10000.0
SSD_DINNER = 2 * D_MODEL
SSD_HEADDIM = 64
SSD_HEADS = SSD_DINNER // SSD_HEADDIM
SSD_GROUPS = 8
SSD_HPG = SSD_HEADS // SSD_GROUPS
SSD_STATE = 128
SSD_CONV = 4
SSD_BC = SSD_GROUPS * SSD_STATE
SSD_XBC = SSD_DINNER + 2 * SSD_BC
SSD_NORM_GROUP = SSD_DINNER // SSD_GROUPS
D_FF = 2816
LOG2E = 1.4426950408889634

COL_Q = 0
COL_K = RET_QK
COL_V = 2 * RET_QK
COL_G = COL_V + RET_V
COL_Z = COL_G + RET_V
COL_XBC = COL_Z + SSD_DINNER
COL_DT = COL_XBC + SSD_XBC
COL_GATES = COL_DT + SSD_HEADS
PROJ_TN = 1024
HALO = 16
LEAD = 32
SEG_PAD = 8
DT_W = 2 * SSD_HEADS

VMEM_LIMIT = 56 * 1024 * 1024


def _dot(a, b):
    return jnp.dot(a, b, preferred_element_type=F32)


def _dot_nt(a, b):
    return lax.dot_general(a, b, (((1,), (1,)), ((), ())), preferred_element_type=F32)


def _dot_tn(a, b):
    return lax.dot_general(a, b, (((0,), (0,)), ((), ())), preferred_element_type=F32)


def _sigmoid(x):
    return 0.5 * jnp.tanh(0.5 * x) + 0.5


def _silu(x):
    h = 0.5 * x
    return h * jnp.tanh(h) + h


def _split2(v):
    hi = v.astype(BF16)
    lo = (v - hi.astype(F32)).astype(BF16)
    return hi, lo


def _split3(v):
    hi = v.astype(BF16)
    r = v - hi.astype(F32)
    mid = r.astype(BF16)
    lo = (r - mid.astype(F32)).astype(BF16)
    return hi, mid, lo


def _rms(v, gain):
    return v * lax.rsqrt(jnp.mean(v * v, axis=-1, keepdims=True) + EPS) * gain


def _params(*sem):
    return pltpu.CompilerParams(dimension_semantics=sem, vmem_limit_bytes=VMEM_LIMIT)


def _lane_blocks(ref_or_val, rows, width):
    return [ref_or_val[rows, c * 128:(c + 1) * 128] for c in range(width // 128)]


def _norm_kernel(*refs, tm, seq_tiles, permute):
    if permute:
        x_ref, g_ref, wdt_ref, prev_ref, meta_ref, hn_ref, dt_ref, hnp_ref, stg_ref = refs
    else:
        x_ref, g_ref, wdt_ref, hn_ref, dt_ref = refs
    hn_f = _rms(x_ref[...], g_ref[...])
    hn = hn_f.astype(BF16)
    hn_ref[...] = hn
    dt_ref[...] = _dot(hn, wdt_ref[...])[:, :DT_W]
    if permute:
        seg = tm // 8
        groups = seg // 8
        back = LEAD // 8
        first = (pl.program_id(0) % seq_tiles) == 0
        left_f = _rms(jnp.where(first, meta_ref[N_META - 8:, :], prev_ref[...]), g_ref[...])
        for i in range(8):
            src = left_f if i == 0 else hn_f[i * seg - 8:i * seg, :]
            for c, blk in enumerate(_lane_blocks(src, slice(8 - back, 8), D_MODEL)):
                stg_ref[c, pl.ds(i, back, stride=8), :] = blk
        for m in range(tm // 8):
            i, j0 = m // groups, 8 * (m % groups)
            for c, blk in enumerate(_lane_blocks(hn_f, slice(8 * m, 8 * m + 8), D_MODEL)):
                stg_ref[c, pl.ds(LEAD + 8 * j0 + i, 8, stride=8), :] = blk
        hnp_ref[...] = jnp.concatenate(
            [stg_ref[c] for c in range(D_MODEL // 128)], axis=1).astype(BF16)


def _prenorm(x2d, gain, w_dt, tm, rows_per_seq, meta=None):
    m = x2d.shape[0]
    permute = meta is not None
    assert m % tm == 0 and rows_per_seq % tm == 0
    row = lambda i: (i, 0)
    const = lambda i: (0, 0)
    in_specs = [pl.BlockSpec((tm, D_MODEL), row), pl.BlockSpec((1, D_MODEL), const),
                pl.BlockSpec((D_MODEL, 128), const)]
    args = [x2d, gain, w_dt]
    out_specs = [pl.BlockSpec((tm, D_MODEL), row), pl.BlockSpec((tm, DT_W), row)]
    out_shape = [jax.ShapeDtypeStruct((m, D_MODEL), BF16), jax.ShapeDtypeStruct((m, DT_W), F32)]
    scratch = []
    if permute:
        assert tm % 64 == 0
        in_specs += [pl.BlockSpec((8, D_MODEL), lambda i: (jnp.maximum(i * (tm // 8) - 1, 0), 0)),
                     pl.BlockSpec((N_META, D_MODEL), const)]
        args += [x2d, meta]
        out_specs.append(pl.BlockSpec((tm + LEAD, D_MODEL), row))
        out_shape.append(jax.ShapeDtypeStruct((m // tm * (tm + LEAD), D_MODEL), BF16))
        scratch.append(pltpu.VMEM((D_MODEL // 128, tm + LEAD, 128), F32))
    return pl.pallas_call(
        functools.partial(_norm_kernel, tm=tm, seq_tiles=rows_per_seq // tm, permute=permute),
        grid=(m // tm,),
        in_specs=in_specs,
        out_specs=out_specs,
        out_shape=out_shape,
        scratch_shapes=scratch,
        compiler_params=_params("parallel"),
        name="prenorm",
    )(*args)


def _rotary(acc, cos, sin, scale):
    half = RET_DK // 2
    cos = cos * scale
    sin = sin * scale
    parts = []
    for h in range(RET_HEADS):
        t1 = acc[:, h * RET_DK:h * RET_DK + half]
        t2 = acc[:, h * RET_DK + half:(h + 1) * RET_DK]
        parts += [t1 * cos - t2 * sin, t1 * sin + t2 * cos]
    return jnp.concatenate(parts, axis=1)


MAIN_PLAN = ((("rotary_q", 0, 0), ("rotary_k", 0, 1))
             + tuple(("identity", 1, j) for j in range(RET_V // PROJ_TN))
             + tuple(("silu", 2, j) for j in range((RET_V + SSD_DINNER) // PROJ_TN))
             + tuple(("sigmoid", 3, j) for j in range(2 * D_MODEL // PROJ_TN)))
MAIN_WIDTHS = (2 * RET_QK, RET_V, RET_V + SSD_DINNER, 2 * D_MODEL)


def _main_kernel(hn_ref, w_ref, cos_ref, sin_ref, *o_refs):
    for jt, (kind, oi, oc) in enumerate(MAIN_PLAN):
        acc = _dot(hn_ref[...], w_ref[:, jt * PROJ_TN:(jt + 1) * PROJ_TN])
        if kind == "rotary_q":
            acc = _rotary(acc, cos_ref[...], sin_ref[...], 1.0)
        elif kind == "rotary_k":
            acc = _rotary(acc, cos_ref[...], sin_ref[...], RET_DK ** -0.5)
        elif kind == "silu":
            acc = _silu(acc)
        elif kind == "sigmoid":
            acc = _sigmoid(acc)
        else:
            assert kind == "identity"
        o_refs[oi][:, oc * PROJ_TN:(oc + 1) * PROJ_TN] = acc.astype(BF16)


def _main_projection(hn, w, cos, sin, tm):
    m = hn.shape[0]
    rows_per_seq = cos.shape[0]
    assert m % tm == 0 and rows_per_seq % tm == 0 and w.shape[1] == sum(MAIN_WIDTHS)
    seq_tiles = rows_per_seq // tm
    row = lambda i: (i, 0)
    pos = pl.BlockSpec((tm, RET_DK // 2), lambda i: (i % seq_tiles, 0))
    return pl.pallas_call(
        _main_kernel,
        grid=(m // tm,),
        in_specs=[pl.BlockSpec((tm, D_MODEL), row),
                  pl.BlockSpec(w.shape, lambda i: (0, 0), pipeline_mode=pl.Buffered(1)), pos, pos],
        out_specs=[pl.BlockSpec((tm, n), row) for n in MAIN_WIDTHS],
        out_shape=[jax.ShapeDtypeStruct((m, n), BF16) for n in MAIN_WIDTHS],
        compiler_params=_params("parallel"),
        name="proj_main",
    )(hn, w, cos, sin)


def _lin_kernel(*refs, kind, tm, n, seq_tiles):
    ncol = n // PROJ_TN
    cols = lambda jt: slice(jt * PROJ_TN, (jt + 1) * PROJ_TN)
    if kind == "conv":
        hn_ref, w_ref, halo_ref, meta_ref, cw_ref, cb_ref, o_ref = refs
        first = (pl.program_id(0) % seq_tiles) == 0
        lhs = jnp.concatenate([jnp.where(first, meta_ref[...], halo_ref[...]), hn_ref[...]], axis=0)
        rows = tm + 8
        for jt in range(ncol):
            acc = _dot(lhs, w_ref[:, cols(jt)])
            full = acc[HALO - 8:, :]
            out = cb_ref[:, cols(jt)] + cw_ref[SSD_CONV - 1:SSD_CONV, cols(jt)] * full[8:, :]
            for k in range(SSD_CONV - 1):
                back = SSD_CONV - 1 - k
                win = pltpu.roll(full, rows - (8 - back), axis=0)[:tm, :]
                out = out + cw_ref[k:k + 1, cols(jt)] * win
            o_ref[:, cols(jt)] = _silu(out).astype(BF16)
    elif kind == "conv_interleaved":
        hnp_ref, w_ref, cw_ref, cb_ref, o_ref, stg_ref, acc_ref = refs
        seg = tm // 8
        dyn0 = jnp.minimum(pl.program_id(0), 0)
        for jt in range(ncol):
            slot = jt % 2 + dyn0
            acc_ref[slot] = _dot(hnp_ref[...], w_ref[:, cols(jt)])
            out = cb_ref[:, cols(jt)]
            for back in range(SSD_CONV):
                k = SSD_CONV - 1 - back
                out = out + cw_ref[k:k + 1, cols(jt)] * acc_ref[slot, LEAD - 8 * back:LEAD - 8 * back + tm, :]
            act = _silu(out)
            for j in range(seg):
                for c, blk in enumerate(_lane_blocks(act, slice(8 * j, 8 * j + 8), PROJ_TN)):
                    stg_ref[jt % 2, c, pl.ds(j, 8, stride=seg + SEG_PAD), :] = blk
            for i in range(8):
                rows = slice((seg + SEG_PAD) * i, (seg + SEG_PAD) * i + seg)
                o_ref[seg * i:seg * (i + 1), cols(jt)] = jnp.concatenate(
                    [stg_ref[jt % 2, c, rows, :] for c in range(PROJ_TN // 128)], axis=1).astype(BF16)
    else:
        raise ValueError(kind)


def _linear(hn, w, kind, tm, rows_per_seq, extra=()):
    interleaved = kind == "conv_interleaved"
    rows_in = tm + LEAD if interleaved else tm
    assert hn.shape[0] % rows_in == 0
    m = hn.shape[0] // rows_in * tm
    n = w.shape[1]
    assert rows_per_seq % tm == 0 and n % PROJ_TN == 0 and tm % HALO == 0
    seq_tiles = rows_per_seq // tm
    const = lambda i: (0, 0)
    resident = functools.partial(pl.BlockSpec, index_map=const, pipeline_mode=pl.Buffered(1))
    in_specs = [pl.BlockSpec((rows_in, D_MODEL), lambda i: (i, 0)), resident((D_MODEL, n))]
    args = [hn, w]
    if kind == "conv":
        meta_hn, conv_w, conv_b = extra
        in_specs += [
            pl.BlockSpec((HALO, D_MODEL), lambda i: (jnp.maximum(i * (tm // HALO) - 1, 0), 0)),
            pl.BlockSpec((HALO, D_MODEL), const),
            pl.BlockSpec((SSD_CONV, n), const),
            pl.BlockSpec((1, n), const),
        ]
        args += [hn, meta_hn, conv_w, conv_b]
    elif interleaved:
        in_specs += [pl.BlockSpec((SSD_CONV, n), const), pl.BlockSpec((1, n), const)]
        args += list(extra)
    scratch = []
    if interleaved:
        assert tm % 64 == 0
        scratch.append(pltpu.VMEM((2, PROJ_TN // 128, 8 * (tm // 8 + SEG_PAD), 128), F32))
        scratch.append(pltpu.VMEM((2, tm + LEAD, PROJ_TN), F32))
    return pl.pallas_call(
        functools.partial(_lin_kernel, kind=kind, tm=tm, n=n, seq_tiles=seq_tiles),
        grid=(m // tm,),
        in_specs=in_specs,
        out_specs=pl.BlockSpec((tm, n), lambda i: (i, 0)),
        out_shape=jax.ShapeDtypeStruct((m, n), BF16),
        scratch_shapes=scratch,
        compiler_params=_params("parallel"),
        name="proj_" + kind,
    )(*args)


def _ret_tables(t):
    log_g = np.log1p(-np.exp2(-5.0 - np.arange(RET_HEADS, dtype=np.float64)))
    idx = np.arange(t, dtype=np.float64)
    diff = idx[:, None] - idx[None, :]
    dmask = np.where(diff >= 0, np.exp(log_g[:, None, None] * np.maximum(diff, 0.0)), 0.0)
    xi = np.exp(log_g[:, None] * (idx[None, :] + 1.0))
    zeta = np.exp(log_g[:, None] * (t - 1.0 - idx[None, :]))
    g_chunk = np.exp(log_g * t)
    xi = np.broadcast_to(xi[:, :, None], (RET_HEADS, t, RET_DK))
    zeta = np.broadcast_to(zeta[:, :, None], (RET_HEADS, t, RET_DK))
    return (jnp.asarray(dmask, F32), jnp.asarray(xi, F32), jnp.asarray(zeta, F32),
            tuple(float(v) for v in g_chunk))


def _ret_kernel(*refs, g_chunk, emit_out):
    if emit_out:
        (q_ref, k_ref, v_ref, st0_ref, dmask_ref, xi_ref, zeta_ref, g_ref, w_ref,
         o_ref, st_ref, ret_ref) = refs
    else:
        (q_ref, k_ref, v_ref, st0_ref, dmask_ref, xi_ref, zeta_ref, stout_ref, st_ref) = refs
    c = pl.program_id(1)

    @pl.when(c == 0)
    def _():
        st_ref[...] = st0_ref[...]

    for h in range(RET_HEADS):
        qh = q_ref[0, :, h * RET_DK:(h + 1) * RET_DK]
        kh = k_ref[0, :, h * RET_DK:(h + 1) * RET_DK]
        vh = v_ref[0, :, h * RET_DV:(h + 1) * RET_DV]
        s = _dot_nt(qh, kh) * dmask_ref[h]
        qx = (qh.astype(F32) * xi_ref[h]).astype(BF16)
        kz = (kh.astype(F32) * zeta_ref[h]).astype(BF16)
        st = st_ref[h]
        if emit_out:
            out = _dot(s.astype(BF16), vh) + _dot(qx, st.astype(BF16))
            r = out * lax.rsqrt(jnp.mean(out * out, axis=-1, keepdims=True) + EPS)
            gate = g_ref[0, :, h * RET_DV:(h + 1) * RET_DV].astype(F32)
            ret_ref[:, h * RET_DV:(h + 1) * RET_DV] = (r * gate).astype(BF16)
        st_ref[h] = st * g_chunk[h] + _dot_tn(kz, vh)

    if emit_out:
        o_ref[0] = _dot(ret_ref[...], w_ref[...]).astype(BF16)
    else:
        @pl.when(c == pl.num_programs(1) - 1)
        def _():
            stout_ref[...] = st_ref[...]


def _retention(qk3d, v3d, gz3d, st0, w_ret, t, emit_out):
    b, l, _ = qk3d.shape
    assert l % t == 0
    nc = l // t
    dmask, xi, zeta, g_chunk = _ret_tables(t)
    const3 = lambda bb, cc: (0, 0, 0)
    in_specs = [
        pl.BlockSpec((1, t, RET_QK), lambda bb, cc: (bb, cc, 0)),
        pl.BlockSpec((1, t, RET_QK), lambda bb, cc: (bb, cc, 1)),
        pl.BlockSpec((1, t, RET_V), lambda bb, cc: (bb, cc, 0)),
        pl.BlockSpec((RET_HEADS, RET_DK, RET_DV), const3),
        pl.BlockSpec((RET_HEADS, t, t), const3),
        pl.BlockSpec((RET_HEADS, t, RET_DK), const3),
        pl.BlockSpec((RET_HEADS, t, RET_DK), const3),
    ]
    args = [qk3d, qk3d, v3d, st0, dmask, xi, zeta]
    scratch = [pltpu.VMEM((RET_HEADS, RET_DK, RET_DV), F32)]
    if emit_out:
        in_specs += [
            pl.BlockSpec((1, t, RET_V), lambda bb, cc: (bb, cc, 0)),
            pl.BlockSpec((RET_V, D_MODEL), lambda bb, cc: (0, 0)),
        ]
        args += [gz3d, w_ret]
        out_specs = pl.BlockSpec((1, t, D_MODEL), lambda bb, cc: (bb, cc, 0))
        out_shape = jax.ShapeDtypeStruct((b, l, D_MODEL), BF16)
        scratch.append(pltpu.VMEM((t, RET_V), BF16))
    else:
        assert b == 1
        out_specs = pl.BlockSpec((RET_HEADS, RET_DK, RET_DV), const3)
        out_shape = jax.ShapeDtypeStruct((RET_HEADS, RET_DK, RET_DV), F32)
    return pl.pallas_call(
        functools.partial(_ret_kernel, g_chunk=g_chunk, emit_out=emit_out),
        grid=(b, nc),
        in_specs=in_specs,
        out_specs=out_specs,
        out_shape=out_shape,
        scratch_shapes=scratch,
        compiler_params=_params("parallel", "arbitrary"),
        name="retention" if emit_out else "retention_meta",
    )(*args)


def _ssd_kernel(*refs, t, emit_out):
    if emit_out:
        (xbc_ref, dt_ref, st0_ref, dtb_ref, alog_ref, e_ref,
         z_ref, dskip_ref, norm_ref, w_ref, o_ref, st_ref, y_ref) = refs
    else:
        (xbc_ref, dt_ref, st0_ref, dtb_ref, alog_ref, e_ref, stout_ref, st_ref) = refs
    c = pl.program_id(1)

    @pl.when(c == 0)
    def _():
        st_ref[...] = st0_ref[...]

    dt_raw = dt_ref[0] + dtb_ref[...]
    dt = jnp.maximum(dt_raw, 0.0) + jnp.log1p(jnp.exp(-jnp.abs(dt_raw)))
    da = dt * (-jnp.exp(alog_ref[...]))
    row = lax.broadcasted_iota(jnp.int32, (t, t), 0)
    col = lax.broadcasted_iota(jnp.int32, (t, t), 1)
    causal = row >= col
    tril = jnp.where(causal, 1.0, 0.0).astype(BF16)
    cs = sum(_dot(tril, part) for part in _split3(da))
    ecs = jnp.exp(cs)
    edl = jnp.exp(cs[t - 1:t, :] - cs)
    cs2 = cs * LOG2E
    cs2_t = cs2.T
    lane = lax.broadcasted_iota(jnp.int32, (t, DT_W), 1)

    def expand(v):
        hi, lo = _split2(v)
        return _dot(jnp.where(lane < SSD_HEADS, hi, lo), e_ref[...])

    xs = xbc_ref[0, :, :SSD_DINNER].astype(F32)
    xdt_f = xs * expand(dt)
    xdt = xdt_f.astype(BF16)
    xdec = (xdt_f * expand(edl)).astype(BF16)
    ecs_e = expand(ecs)
    head_of_col = lax.broadcasted_iota(jnp.int32, (t, SSD_NORM_GROUP), 1) // SSD_HEADDIM
    zeros_b = jnp.zeros((t, SSD_STATE), BF16)

    def b_cols(g):
        return xbc_ref[0, :, SSD_DINNER + g * SSD_STATE:SSD_DINNER + (g + 1) * SSD_STATE]

    for pair in range(SSD_GROUPS // 2):
        if emit_out:
            c2 = xbc_ref[0, :, SSD_DINNER + SSD_BC + 2 * pair * SSD_STATE:
                         SSD_DINNER + SSD_BC + (2 * pair + 2) * SSD_STATE]
            b2 = jnp.concatenate([jnp.concatenate([b_cols(2 * pair), zeros_b], axis=1),
                                  jnp.concatenate([zeros_b, b_cols(2 * pair + 1)], axis=1)], axis=0)
            cb2 = _dot_nt(c2, b2)
        for half in range(2):
            g = 2 * pair + half
            gs = slice(g * SSD_NORM_GROUP, (g + 1) * SSD_NORM_GROUP)
            bg = b_cols(g)
            stg = st_ref[:, gs]
            if emit_out:
                cg = c2[:, half * SSD_STATE:(half + 1) * SSD_STATE]
                cb = jnp.where(causal, cb2[:, half * t:(half + 1) * t], 0.0)
                xg = xdt[:, gs]
                ms, xm = [], []
                for h in range(SSD_HPG):
                    hh = g * SSD_HPG + h
                    seg = jnp.minimum(cs2[:, hh:hh + 1] - cs2_t[hh:hh + 1, :], 0.0)
                    ms.append((cb * jnp.exp2(seg)).astype(BF16))
                    xm.append(jnp.where(head_of_col == h, xg, jnp.zeros_like(xg)))
                y = _dot(jnp.concatenate(ms, axis=1), jnp.concatenate(xm, axis=0))
                y = y + _dot(cg, stg.astype(BF16)) * ecs_e[:, gs]
                y = y + dskip_ref[:, gs] * xs[:, gs]
                y = y * z_ref[0, :, gs].astype(F32)
                y = y * lax.rsqrt(jnp.mean(y * y, axis=-1, keepdims=True) + EPS)
                y_ref[:, gs] = (y * norm_ref[:, gs]).astype(BF16)
            st_ref[:, gs] = stg * ecs_e[t - 1:t, gs] + _dot_tn(bg, xdec[:, gs])

    if emit_out:
        o_ref[0] = _dot(y_ref[...], w_ref[...]).astype(BF16)
    else:
        @pl.when(c == pl.num_programs(1) - 1)
        def _():
            stout_ref[...] = st_ref[...]


def _ssd(xbc3d, gz3d, dt3d, st0, dt_bias, a_log, expand_mat, dskip_e, ssd_norm, w_ssd, t, emit_out):
    b, l, _ = xbc3d.shape
    assert l % t == 0
    nc = l // t
    const2 = lambda bb, cc: (0, 0)
    in_specs = [
        pl.BlockSpec((1, t, SSD_XBC), lambda bb, cc: (bb, cc, 0)),
        pl.BlockSpec((1, t, DT_W), lambda bb, cc: (bb, cc, 0)),
        pl.BlockSpec((SSD_STATE, SSD_DINNER), const2),
        pl.BlockSpec((1, DT_W), const2),
        pl.BlockSpec((1, DT_W), const2),
        pl.BlockSpec((DT_W, SSD_DINNER), const2),
    ]
    args = [xbc3d, dt3d, st0, dt_bias, a_log, expand_mat]
    scratch = [pltpu.VMEM((SSD_STATE, SSD_DINNER), F32)]
    if emit_out:
        in_specs += [
            pl.BlockSpec((1, t, SSD_DINNER), lambda bb, cc: (bb, cc, RET_V // SSD_DINNER)),
            pl.BlockSpec((1, SSD_DINNER), const2),
            pl.BlockSpec((1, SSD_DINNER), const2),
            pl.BlockSpec((SSD_DINNER, D_MODEL), const2),
        ]
        args += [gz3d, dskip_e, ssd_norm, w_ssd]
        out_specs = pl.BlockSpec((1, t, D_MODEL), lambda bb, cc: (bb, cc, 0))
        out_shape = jax.ShapeDtypeStruct((b, l, D_MODEL), BF16)
        scratch.append(pltpu.VMEM((t, SSD_DINNER), BF16))
    else:
        assert b == 1
        out_specs = pl.BlockSpec((SSD_STATE, SSD_DINNER), const2)
        out_shape = jax.ShapeDtypeStruct((SSD_STATE, SSD_DINNER), F32)
    return pl.pallas_call(
        functools.partial(_ssd_kernel, t=t, emit_out=emit_out),
        grid=(b, nc),
        in_specs=in_specs,
        out_specs=out_specs,
        out_shape=out_shape,
        scratch_shapes=scratch,
        compiler_params=_params("parallel", "arbitrary"),
        name="ssd" if emit_out else "ssd_meta",
    )(*args)


def _tail_kernel(x_ref, ba_ref, bb_ref, gates_ref, wout_ref, npost_ref, nfpre_ref, wg_ref, wu_ref,
                 wd_ref, nfpost_ref, o_ref):
    ga = gates_ref[:, :D_MODEL].astype(F32)
    gb = gates_ref[:, D_MODEL:].astype(F32)
    merged = ga * ba_ref[...].astype(F32) + gb * bb_ref[...].astype(F32)
    mix = _dot(merged.astype(BF16), wout_ref[...])
    h1 = x_ref[...] + _rms(mix, npost_ref[...])
    u = _rms(h1, nfpre_ref[...]).astype(BF16)
    act = (_silu(_dot(u, wg_ref[...])) * _dot(u, wu_ref[...])).astype(BF16)
    f = _dot(act, wd_ref[...])
    o_ref[...] = h1 + _rms(f, nfpost_ref[...])


def _tail(x2d, ba, bb, gates, w_out, n_post, nf_pre, w_gate, w_up, w_down, nf_post, tm):
    m = x2d.shape[0]
    assert m % tm == 0
    row = lambda i: (i, 0)
    const = lambda i: (0, 0)
    resident = functools.partial(pl.BlockSpec, index_map=const, pipeline_mode=pl.Buffered(1))
    return pl.pallas_call(
        _tail_kernel,
        grid=(m // tm,),
        in_specs=[
            pl.BlockSpec((tm, D_MODEL), row),
            pl.BlockSpec((tm, D_MODEL), row),
            pl.BlockSpec((tm, D_MODEL), row),
            pl.BlockSpec((tm, 2 * D_MODEL), row),
            resident((D_MODEL, D_MODEL)),
            pl.BlockSpec((1, D_MODEL), const),
            pl.BlockSpec((1, D_MODEL), const),
            resident((D_MODEL, D_FF)),
            resident((D_MODEL, D_FF)),
            resident((D_FF, D_MODEL)),
            pl.BlockSpec((1, D_MODEL), const),
        ],
        out_specs=pl.BlockSpec((tm, D_MODEL), row),
        out_shape=jax.ShapeDtypeStruct((m, D_MODEL), F32),
        compiler_params=_params("parallel"),
        name="tail",
    )(x2d, ba, bb, gates, w_out, n_post, nf_pre, w_gate, w_up, w_down, nf_post)


PROJ_TM = 512
RET_T = 256
SSD_T = 128
TAIL_TM = 512


def _input_projection(x2d, meta, gain, w_dt, weights, cos, sin, conv_w, conv_b, tm):
    rows_per_seq = cos.shape[0]
    interleave = meta is not None
    hn, dt, *hnp = _prenorm(x2d, gain, w_dt, tm, rows_per_seq, meta)
    qk, v, gz, gates = _main_projection(hn, weights["main"], cos, sin, tm)
    if interleave:
        xbc = _linear(hnp[0], weights["xbc"], "conv_interleaved", tm, rows_per_seq,
                      extra=(conv_w, conv_b))
    else:
        xbc = _linear(hn, weights["xbc"], "conv", tm, rows_per_seq,
                      extra=(jnp.zeros((HALO, D_MODEL), BF16), conv_w, conv_b))
    return dt, qk, v, gz, xbc, gates


def kernel(x, meta_tokens, norm_mix_pre, w_in, conv_w, conv_b, dt_bias, a_log, d_skip, ssd_norm,
           w_ret_branch, w_ssd_branch, w_out, norm_mix_post, norm_ffn_pre, w_gate, w_up, w_down,
           norm_ffn_post):
    b, seq, _ = x.shape
    assert norm_mix_pre.shape[0] == 1, "single-layer block"
    assert meta_tokens.shape[0] == N_META == HALO

    w_in0 = w_in[0]
    weights = {
        "main": jnp.concatenate([w_in0[:, COL_Q:COL_XBC], w_in0[:, COL_GATES:]], axis=1).astype(BF16),
        "xbc": w_in0[:, COL_XBC:COL_DT].astype(BF16),
    }
    w_dt1 = w_in0[:, COL_DT:COL_GATES]
    w_dt = jnp.pad(jnp.concatenate([w_dt1, w_dt1], axis=1), ((0, 0), (0, 128 - DT_W))).astype(BF16)
    w_ret = w_ret_branch[0].astype(BF16)
    w_ssd = w_ssd_branch[0].astype(BF16)
    expand_mat = jnp.tile(jnp.repeat(jnp.eye(SSD_HEADS, dtype=BF16), SSD_HEADDIM, axis=1), (2, 1))
    dskip_e = jnp.repeat(d_skip[0], SSD_HEADDIM)[None, :]
    gain_pre = norm_mix_pre[0][None, :]
    conv_b2 = conv_b[0][None, :]
    dt_bias2 = jnp.tile(dt_bias[0], 2)[None, :]
    a_log2 = jnp.tile(a_log[0], 2)[None, :]
    ssd_norm2 = ssd_norm[0][None, :]

    half = RET_DK // 2
    inv = ROPE_BASE ** (-jnp.arange(half, dtype=F32) / half)
    ang = jnp.arange(N_META + seq, dtype=F32)[:, None] * inv[None, :]
    cos, sin = jnp.cos(ang), jnp.sin(ang)

    meta = meta_tokens.astype(F32)
    dt_m, qk_m, v_m, _, xbc_m, _ = _input_projection(
        meta, None, gain_pre, w_dt, weights, cos[:N_META], sin[:N_META], conv_w[0], conv_b2, N_META)
    ret_st = _retention(qk_m[None], v_m[None], None, jnp.zeros((RET_HEADS, RET_DK, RET_DV), F32),
                        None, N_META, False)
    ssd_st = _ssd(xbc_m[None], None, dt_m[None], jnp.zeros((SSD_STATE, SSD_DINNER), F32), dt_bias2,
                  a_log2, expand_mat, None, None, None, N_META, False)

    x2d = x.reshape(b * seq, D_MODEL)
    dt, qk, v, gz, xbc, gates = _input_projection(
        x2d, meta, gain_pre, w_dt, weights, cos[N_META:], sin[N_META:], conv_w[0], conv_b2, PROJ_TM)
    as3d = lambda a: a.reshape(b, seq, a.shape[-1])
    branch_a = _retention(as3d(qk), as3d(v), as3d(gz), ret_st, w_ret, RET_T, True)
    branch_b = _ssd(as3d(xbc), as3d(gz), as3d(dt), ssd_st, dt_bias2, a_log2, expand_mat, dskip_e,
                    ssd_norm2, w_ssd, SSD_T, True)
    out = _tail(x2d, branch_a.reshape(b * seq, D_MODEL), branch_b.reshape(b * seq, D_MODEL), gates,
                w_out[0].astype(BF16), norm_mix_post[0][None, :], norm_ffn_pre[0][None, :],
                w_gate[0].astype(BF16), w_up[0].astype(BF16), w_down[0].astype(BF16),
                norm_ffn_post[0][None, :], TAIL_TM)
    return out.reshape(b, seq, D_MODEL)
```

```python
import functools

import numpy as np
import jax
import jax.numpy as jnp
from jax import lax
from jax.experimental import pallas as pl
from jax.experimental.pallas import tpu as pltpu

F32 = jnp.float32
BF16 = jnp.bfloat16

D_MODEL = 1024
N_META = 16
EPS = 1e-6
RET_HEADS = 4
RET_DK = 256
RET_DV = 512
RET_QK = RET_HEADS * RET_DK
RET_V = RET_HEADS * RET_DV
ROPE_BASE = 10000.0
SSD_DINNER = 2 * D_MODEL
SSD_HEADDIM = 64
SSD_HEADS = SSD_DINNER // SSD_HEADDIM
SSD_GROUPS = 8
SSD_HPG = SSD_HEADS // SSD_GROUPS
SSD_STATE = 128
SSD_CONV = 4
SSD_BC = SSD_GROUPS * SSD_STATE
SSD_XBC = SSD_DINNER + 2 * SSD_BC
SSD_NORM_GROUP = SSD_DINNER // SSD_GROUPS
D_FF = 2816
LOG2E = 1.4426950408889634

COL_Q = 0
COL_K = RET_QK
COL_V = 2 * RET_QK
COL_G = COL_V + RET_V
COL_Z = COL_G + RET_V
COL_XBC = COL_Z + SSD_DINNER
COL_DT = COL_XBC + SSD_XBC
COL_GATES = COL_DT + SSD_HEADS
PROJ_TN = 1024
HALO = 16
LEAD = 32
SEG_PAD = 8
DT_W = 2 * SSD_HEADS

VMEM_LIMIT = 56 * 1024 * 1024


def _dot(a, b):
    return jnp.dot(a, b, preferred_element_type=F32)


def _dot_nt(a, b):
    return lax.dot_general(a, b, (((1,), (1,)), ((), ())), preferred_element_type=F32)


def _dot_tn(a, b):
    return lax.dot_general(a, b, (((0,), (0,)), ((), ())), preferred_element_type=F32)


def _sigmoid(x):
    return 0.5 * jnp.tanh(0.5 * x) + 0.5


def _silu(x):
    h = 0.5 * x
    return h * jnp.tanh(h) + h


def _split2(v):
    hi = v.astype(BF16)
    lo = (v - hi.astype(F32)).astype(BF16)
    return hi, lo


def _split3(v):
    hi = v.astype(BF16)
    r = v - hi.astype(F32)
    mid = r.astype(BF16)
    lo = (r - mid.astype(F32)).astype(BF16)
    return hi, mid, lo


def _rms(v, gain):
    return v * lax.rsqrt(jnp.mean(v * v, axis=-1, keepdims=True) + EPS) * gain


def _params(*sem):
    return pltpu.CompilerParams(dimension_semantics=sem, vmem_limit_bytes=VMEM_LIMIT)


def _lane_blocks(ref_or_val, rows, width):
    return [ref_or_val[rows, c * 128:(c + 1) * 128] for c in range(width // 128)]


def _norm_kernel(*refs, tm, seq_tiles, permute):
    if permute:
        x_ref, g_ref, wdt_ref, prev_ref, meta_ref, hn_ref, dt_ref, hnp_ref, stg_ref = refs
    else:
        x_ref, g_ref, wdt_ref, hn_ref, dt_ref = refs
    hn_f = _rms(x_ref[...], g_ref[...])
    hn = hn_f.astype(BF16)
    hn_ref[...] = hn
    dt_ref[...] = _dot(hn, wdt_ref[...])[:, :DT_W]
    if permute:
        seg = tm // 8
        groups = seg // 8
        back = LEAD // 8
        first = (pl.program_id(0) % seq_tiles) == 0
        left_f = _rms(jnp.where(first, meta_ref[N_META - 8:, :], prev_ref[...]), g_ref[...])
        for i in range(8):
            src = left_f if i == 0 else hn_f[i * seg - 8:i * seg, :]
            for c, blk in enumerate(_lane_blocks(src, slice(8 - back, 8), D_MODEL)):
                stg_ref[c, pl.ds(i, back, stride=8), :] = blk
        for m in range(tm // 8):
            i, j0 = m // groups, 8 * (m % groups)
            for c, blk in enumerate(_lane_blocks(hn_f, slice(8 * m, 8 * m + 8), D_MODEL)):
                stg_ref[c, pl.ds(LEAD + 8 * j0 + i, 8, stride=8), :] = blk
        hnp_ref[...] = jnp.concatenate(
            [stg_ref[c] for c in range(D_MODEL // 128)], axis=1).astype(BF16)


def _prenorm(x2d, gain, w_dt, tm, rows_per_seq, meta=None):
    m = x2d.shape[0]
    permute = meta is not None
    assert m % tm == 0 and rows_per_seq % tm == 0
    row = lambda i: (i, 0)
    const = lambda i: (0, 0)
    in_specs = [pl.BlockSpec((tm, D_MODEL), row), pl.BlockSpec((1, D_MODEL), const),
                pl.BlockSpec((D_MODEL, 128), const)]
    args = [x2d, gain, w_dt]
    out_specs = [pl.BlockSpec((tm, D_MODEL), row), pl.BlockSpec((tm, DT_W), row)]
    out_shape = [jax.ShapeDtypeStruct((m, D_MODEL), BF16), jax.ShapeDtypeStruct((m, DT_W), F32)]
    scratch = []
    if permute:
        assert tm % 64 == 0
        in_specs += [pl.BlockSpec((8, D_MODEL), lambda i: (jnp.maximum(i * (tm // 8) - 1, 0), 0)),
                     pl.BlockSpec((N_META, D_MODEL), const)]
        args += [x2d, meta]
        out_specs.append(pl.BlockSpec((tm + LEAD, D_MODEL), row))
        out_shape.append(jax.ShapeDtypeStruct((m // tm * (tm + LEAD), D_MODEL), BF16))
        scratch.append(pltpu.VMEM((D_MODEL // 128, tm + LEAD, 128), F32))
    return pl.pallas_call(
        functools.partial(_norm_kernel, tm=tm, seq_tiles=rows_per_seq // tm, permute=permute),
        grid=(m // tm,),
        in_specs=in_specs,
        out_specs=out_specs,
        out_shape=out_shape,
        scratch_shapes=scratch,
        compiler_params=_params("parallel"),
        name="prenorm",
    )(*args)


def _rotary(acc, cos, sin, scale):
    half = RET_DK // 2
    cos = cos * scale
    sin = sin * scale
    parts = []
    for h in range(RET_HEADS):
        t1 = acc[:, h * RET_DK:h * RET_DK + half]
        t2 = acc[:, h * RET_DK + half:(h + 1) * RET_DK]
        parts += [t1 * cos - t2 * sin, t1 * sin + t2 * cos]
    return jnp.concatenate(parts, axis=1)


MAIN_PLAN = ((("rotary_q", 0, 0), ("rotary_k", 0, 1))
             + tuple(("identity", 1, j) for j in range(RET_V // PROJ_TN))
             + tuple(("silu", 2, j) for j in range((RET_V + SSD_DINNER) // PROJ_TN))
             + tuple(("sigmoid", 3, j) for j in range(2 * D_MODEL // PROJ_TN)))
MAIN_WIDTHS = (2 * RET_QK, RET_V, RET_V + SSD_DINNER, 2 * D_MODEL)


def _main_kernel(hn_ref, w_ref, cos_ref, sin_ref, *o_refs):
    for jt, (kind, oi, oc) in enumerate(MAIN_PLAN):
        acc = _dot(hn_ref[...], w_ref[:, jt * PROJ_TN:(jt + 1) * PROJ_TN])
        if kind == "rotary_q":
            acc = _rotary(acc, cos_ref[...], sin_ref[...], 1.0)
        elif kind == "rotary_k":
            acc = _rotary(acc, cos_ref[...], sin_ref[...], RET_DK ** -0.5)
        elif kind == "silu":
            acc = _silu(acc)
        elif kind == "sigmoid":
            acc = _sigmoid(acc)
        else:
            assert kind == "identity"
        o_refs[oi][:, oc * PROJ_TN:(oc + 1) * PROJ_TN] = acc.astype(BF16)


def _main_projection(hn, w, cos, sin, tm):
    m = hn.shape[0]
    rows_per_seq = cos.shape[0]
    assert m % tm == 0 and rows_per_seq % tm == 0 and w.shape[1] == sum(MAIN_WIDTHS)
    seq_tiles = rows_per_seq // tm
    row = lambda i: (i, 0)
    pos = pl.BlockSpec((tm, RET_DK // 2), lambda i: (i % seq_tiles, 0))
    return pl.pallas_call(
        _main_kernel,
        grid=(m // tm,),
        in_specs=[pl.BlockSpec((tm, D_MODEL), row),
                  pl.BlockSpec(w.shape, lambda i: (0, 0), pipeline_mode=pl.Buffered(1)), pos, pos],
        out_specs=[pl.BlockSpec((tm, n), row) for n in MAIN_WIDTHS],
        out_shape=[jax.ShapeDtypeStruct((m, n), BF16) for n in MAIN_WIDTHS],
        compiler_params=_params("parallel"),
        name="proj_main",
    )(hn, w, cos, sin)


def _lin_kernel(*refs, kind, tm, n, seq_tiles):
    ncol = n // PROJ_TN
    cols = lambda jt: slice(jt * PROJ_TN, (jt + 1) * PROJ_TN)
    if kind == "conv":
        hn_ref, w_ref, halo_ref, meta_ref, cw_ref, cb_ref, o_ref = refs
        first = (pl.program_id(0) % seq_tiles) == 0
        lhs = jnp.concatenate([jnp.where(first, meta_ref[...], halo_ref[...]), hn_ref[...]], axis=0)
        rows = tm + 8
        for jt in range(ncol):
            acc = _dot(lhs, w_ref[:, cols(jt)])
            full = acc[HALO - 8:, :]
            out = cb_ref[:, cols(jt)] + cw_ref[SSD_CONV - 1:SSD_CONV, cols(jt)] * full[8:, :]
            for k in range(SSD_CONV - 1):
                back = SSD_CONV - 1 - k
                win = pltpu.roll(full, rows - (8 - back), axis=0)[:tm, :]
                out = out + cw_ref[k:k + 1, cols(jt)] * win
            o_ref[:, cols(jt)] = _silu(out).astype(BF16)
    elif kind == "conv_interleaved":
        hnp_ref, w_ref, cw_ref, cb_ref, o_ref, stg_ref, acc_ref = refs
        seg = tm // 8
        dyn0 = jnp.minimum(pl.program_id(0), 0)
        for jt in range(ncol):
            slot = jt % 2 + dyn0
            acc_ref[slot] = _dot(hnp_ref[...], w_ref[:, cols(jt)])
            out = cb_ref[:, cols(jt)]
            for back in range(SSD_CONV):
                k = SSD_CONV - 1 - back
                out = out + cw_ref[k:k + 1, cols(jt)] * acc_ref[slot, LEAD - 8 * back:LEAD - 8 * back + tm, :]
            act = _silu(out)
            for j in range(seg):
                for c, blk in enumerate(_lane_blocks(act, slice(8 * j, 8 * j + 8), PROJ_TN)):
                    stg_ref[jt % 2, c, pl.ds(j, 8, stride=seg + SEG_PAD), :] = blk
            for i in range(8):
                rows = slice((seg + SEG_PAD) * i, (seg + SEG_PAD) * i + seg)
                o_ref[seg * i:seg * (i + 1), cols(jt)] = jnp.concatenate(
                    [stg_ref[jt % 2, c, rows, :] for c in range(PROJ_TN // 128)], axis=1).astype(BF16)
    else:
        raise ValueError(kind)


def _linear(hn, w, kind, tm, rows_per_seq, extra=()):
    interleaved = kind == "conv_interleaved"
    rows_in = tm + LEAD if interleaved else tm
    assert hn.shape[0] % rows_in == 0
    m = hn.shape[0] // rows_in * tm
    n = w.shape[1]
    assert rows_per_seq % tm == 0 and n % PROJ_TN == 0 and tm % HALO == 0
    seq_tiles = rows_per_seq // tm
    const = lambda i: (0, 0)
    resident = functools.partial(pl.BlockSpec, index_map=const, pipeline_mode=pl.Buffered(1))
    in_specs = [pl.BlockSpec((rows_in, D_MODEL), lambda i: (i, 0)), resident((D_MODEL, n))]
    args = [hn, w]
    if kind == "conv":
        meta_hn, conv_w, conv_b = extra
        in_specs += [
            pl.BlockSpec((HALO, D_MODEL), lambda i: (jnp.maximum(i * (tm // HALO) - 1, 0), 0)),
            pl.BlockSpec((HALO, D_MODEL), const),
            pl.BlockSpec((SSD_CONV, n), const),
            pl.BlockSpec((1, n), const),
        ]
        args += [hn, meta_hn, conv_w, conv_b]
    elif interleaved:
        in_specs += [pl.BlockSpec((SSD_CONV, n), const), pl.BlockSpec((1, n), const)]
        args += list(extra)
    scratch = []
    if interleaved:
        assert tm % 64 == 0
        scratch.append(pltpu.VMEM((2, PROJ_TN // 128, 8 * (tm // 8 + SEG_PAD), 128), F32))
        scratch.append(pltpu.VMEM((2, tm + LEAD, PROJ_TN), F32))
    return pl.pallas_call(
        functools.partial(_lin_kernel, kind=kind, tm=tm, n=n, seq_tiles=seq_tiles),
        grid=(m // tm,),
        in_specs=in_specs,
        out_specs=pl.BlockSpec((tm, n), lambda i: (i, 0)),
        out_shape=jax.ShapeDtypeStruct((m, n), BF16),
        scratch_shapes=scratch,
        compiler_params=_params("parallel"),
        name="proj_" + kind,
    )(*args)


def _ret_tables(t):
    log_g = np.log1p(-np.exp2(-5.0 - np.arange(RET_HEADS, dtype=np.float64)))
    idx = np.arange(t, dtype=np.float64)
    diff = idx[:, None] - idx[None, :]
    dmask = np.where(diff >= 0, np.exp(log_g[:, None, None] * np.maximum(diff, 0.0)), 0.0)
    xi = np.exp(log_g[:, None] * (idx[None, :] + 1.0))
    zeta = np.exp(log_g[:, None] * (t - 1.0 - idx[None, :]))
    g_chunk = np.exp(log_g * t)
    xi = np.broadcast_to(xi[:, :, None], (RET_HEADS, t, RET_DK))
    zeta = np.broadcast_to(zeta[:, :, None], (RET_HEADS, t, RET_DK))
    return (jnp.asarray(dmask, F32), jnp.asarray(xi, F32), jnp.asarray(zeta, F32),
            tuple(float(v) for v in g_chunk))


def _ret_kernel(*refs, g_chunk, emit_out):
    if emit_out:
        (q_ref, k_ref, v_ref, st0_ref, dmask_ref, xi_ref, zeta_ref, g_ref, w_ref,
         o_ref, st_ref, ret_ref) = refs
    else:
        (q_ref, k_ref, v_ref, st0_ref, dmask_ref, xi_ref, zeta_ref, stout_ref, st_ref) = refs
    c = pl.program_id(1)

    @pl.when(c == 0)
    def _():
        st_ref[...] = st0_ref[...]

    for h in range(RET_HEADS):
        qh = q_ref[0, :, h * RET_DK:(h + 1) * RET_DK]
        kh = k_ref[0, :, h * RET_DK:(h + 1) * RET_DK]
        vh = v_ref[0, :, h * RET_DV:(h + 1) * RET_DV]
        s = _dot_nt(qh, kh) * dmask_ref[h]
        qx = (qh.astype(F32) * xi_ref[h]).astype(BF16)
        kz = (kh.astype(F32) * zeta_ref[h]).astype(BF16)
        st = st_ref[h]
        if emit_out:
            out = _dot(s.astype(BF16), vh) + _dot(qx, st.astype(BF16))
            r = out * lax.rsqrt(jnp.mean(out * out, axis=-1, keepdims=True) + EPS)
            gate = g_ref[0, :, h * RET_DV:(h + 1) * RET_DV].astype(F32)
            ret_ref[:, h * RET_DV:(h + 1) * RET_DV] = (r * gate).astype(BF16)
        st_ref[h] = st * g_chunk[h] + _dot_tn(kz, vh)

    if emit_out:
        o_ref[0] = _dot(ret_ref[...], w_ref[...]).astype(BF16)
    else:
        @pl.when(c == pl.num_programs(1) - 1)
        def _():
            stout_ref[...] = st_ref[...]


def _retention(qk3d, v3d, gz3d, st0, w_ret, t, emit_out):
    b, l, _ = qk3d.shape
    assert l % t == 0
    nc = l // t
    dmask, xi, zeta, g_chunk = _ret_tables(t)
    const3 = lambda bb, cc: (0, 0, 0)
    in_specs = [
        pl.BlockSpec((1, t, RET_QK), lambda bb, cc: (bb, cc, 0)),
        pl.BlockSpec((1, t, RET_QK), lambda bb, cc: (bb, cc, 1)),
        pl.BlockSpec((1, t, RET_V), lambda bb, cc: (bb, cc, 0)),
        pl.BlockSpec((RET_HEADS, RET_DK, RET_DV), const3),
        pl.BlockSpec((RET_HEADS, t, t), const3),
        pl.BlockSpec((RET_HEADS, t, RET_DK), const3),
        pl.BlockSpec((RET_HEADS, t, RET_DK), const3),
    ]
    args = [qk3d, qk3d, v3d, st0, dmask, xi, zeta]
    scratch = [pltpu.VMEM((RET_HEADS, RET_DK, RET_DV), F32)]
    if emit_out:
        in_specs += [
            pl.BlockSpec((1, t, RET_V), lambda bb, cc: (bb, cc, 0)),
            pl.BlockSpec((RET_V, D_MODEL), lambda bb, cc: (0, 0)),
        ]
        args += [gz3d, w_ret]
        out_specs = pl.BlockSpec((1, t, D_MODEL), lambda bb, cc: (bb, cc, 0))
        out_shape = jax.ShapeDtypeStruct((b, l, D_MODEL), BF16)
        scratch.append(pltpu.VMEM((t, RET_V), BF16))
    else:
        assert b == 1
        out_specs = pl.BlockSpec((RET_HEADS, RET_DK, RET_DV), const3)
        out_shape = jax.ShapeDtypeStruct((RET_HEADS, RET_DK, RET_DV), F32)
    return pl.pallas_call(
        functools.partial(_ret_kernel, g_chunk=g_chunk, emit_out=emit_out),
        grid=(b, nc),
        in_specs=in_specs,
        out_specs=out_specs,
        out_shape=out_shape,
        scratch_shapes=scratch,
        compiler_params=_params("parallel", "arbitrary"),
        name="retention" if emit_out else "retention_meta",
    )(*args)


def _ssd_kernel(*refs, t, nseq, emit_out):
    if emit_out:
        (xbc_ref, dt_ref, st0_ref, dtb_ref, alog_ref, e_ref,
         z_ref, dskip_ref, norm_ref, w_ref, o_ref, st_ref, y_ref) = refs
    else:
        (xbc_ref, dt_ref, st0_ref, dtb_ref, alog_ref, e_ref, stout_ref, st_ref) = refs
    c = pl.program_id(1)

    @pl.when(c == 0)
    def _():
        for q in range(nseq):
            st_ref[q] = st0_ref[...]

    row = lax.broadcasted_iota(jnp.int32, (t, t), 0)
    col = lax.broadcasted_iota(jnp.int32, (t, t), 1)
    causal = row >= col
    tril = jnp.where(causal, 1.0, 0.0).astype(BF16)
    lane = lax.broadcasted_iota(jnp.int32, (t, DT_W), 1)
    head_of_col = lax.broadcasted_iota(jnp.int32, (t, SSD_NORM_GROUP), 1) // SSD_HEADDIM
    zeros_b = jnp.zeros((t, SSD_STATE), BF16)

    for q in range(nseq):
        dt_raw = dt_ref[q] + dtb_ref[...]
        dt = jnp.maximum(dt_raw, 0.0) + jnp.log1p(jnp.exp(-jnp.abs(dt_raw)))
        da = dt * (-jnp.exp(alog_ref[...]))
        cs = sum(_dot(tril, part) for part in _split3(da))
        ecs = jnp.exp(cs)
        edl = jnp.exp(cs[t - 1:t, :] - cs)
        cs2 = cs * LOG2E
        cs2_t = cs2.T

        def expand(v):
            hi, lo = _split2(v)
            return _dot(jnp.where(lane < SSD_HEADS, hi, lo), e_ref[...])

        xs = xbc_ref[q, :, :SSD_DINNER].astype(F32)
        xdt_f = xs * expand(dt)
        xdt = xdt_f.astype(BF16)
        xdec = (xdt_f * expand(edl)).astype(BF16)
        ecs_e = expand(ecs)

        def b_cols(g):
            return xbc_ref[q, :, SSD_DINNER + g * SSD_STATE:SSD_DINNER + (g + 1) * SSD_STATE]

        for pair in range(SSD_GROUPS // 2):
            if emit_out:
                c2 = xbc_ref[q, :, SSD_DINNER + SSD_BC + 2 * pair * SSD_STATE:
                             SSD_DINNER + SSD_BC + (2 * pair + 2) * SSD_STATE]
                b2 = jnp.concatenate([jnp.concatenate([b_cols(2 * pair), zeros_b], axis=1),
                                      jnp.concatenate([zeros_b, b_cols(2 * pair + 1)], axis=1)], axis=0)
                cb2 = _dot_nt(c2, b2)
            for half in range(2):
                g = 2 * pair + half
                gs = slice(g * SSD_NORM_GROUP, (g + 1) * SSD_NORM_GROUP)
                bg = b_cols(g)
                stg = st_ref[q, :, gs]
                if emit_out:
                    cg = c2[:, half * SSD_STATE:(half + 1) * SSD_STATE]
                    cb = jnp.where(causal, cb2[:, half * t:(half + 1) * t], 0.0)
                    xg = xdt[:, gs]
                    ms, xm = [], []
                    for h in range(SSD_HPG):
                        hh = g * SSD_HPG + h
                        seg = jnp.minimum(cs2[:, hh:hh + 1] - cs2_t[hh:hh + 1, :], 0.0)
                        ms.append((cb * jnp.exp2(seg)).astype(BF16))
                        xm.append(jnp.where(head_of_col == h, xg, jnp.zeros_like(xg)))
                    y = _dot(jnp.concatenate(ms, axis=1), jnp.concatenate(xm, axis=0))
                    y = y + _dot(cg, stg.astype(BF16)) * ecs_e[:, gs]
                    y = y + dskip_ref[:, gs] * xs[:, gs]
                    y = y * z_ref[q, :, gs].astype(F32)
                    y = y * lax.rsqrt(jnp.mean(y * y, axis=-1, keepdims=True) + EPS)
                    y_ref[q * t:(q + 1) * t, gs] = (y * norm_ref[:, gs]).astype(BF16)
                st_ref[q, :, gs] = stg * ecs_e[t - 1:t, gs] + _dot_tn(bg, xdec[:, gs])

    if emit_out:
        out = _dot(y_ref[...], w_ref[...]).astype(BF16)
        for q in range(nseq):
            o_ref[q] = out[q * t:(q + 1) * t, :]
    else:
        @pl.when(c == pl.num_programs(1) - 1)
        def _():
            stout_ref[...] = st_ref[0]


def _ssd(xbc3d, gz3d, dt3d, st0, dt_bias, a_log, expand_mat, dskip_e, ssd_norm, w_ssd, t, nseq, emit_out):
    b, l, _ = xbc3d.shape
    assert l % t == 0 and b % nseq == 0
    nc = l // t
    const2 = lambda bb, cc: (0, 0)
    in_specs = [
        pl.BlockSpec((nseq, t, SSD_XBC), lambda bb, cc: (bb, cc, 0)),
        pl.BlockSpec((nseq, t, DT_W), lambda bb, cc: (bb, cc, 0)),
        pl.BlockSpec((SSD_STATE, SSD_DINNER), const2),
        pl.BlockSpec((1, DT_W), const2),
        pl.BlockSpec((1, DT_W), const2),
        pl.BlockSpec((DT_W, SSD_DINNER), const2),
    ]
    args = [xbc3d, dt3d, st0, dt_bias, a_log, expand_mat]
    scratch = [pltpu.VMEM((nseq, SSD_STATE, SSD_DINNER), F32)]
    if emit_out:
        in_specs += [
            pl.BlockSpec((nseq, t, SSD_DINNER), lambda bb, cc: (bb, cc, RET_V // SSD_DINNER)),
            pl.BlockSpec((1, SSD_DINNER), const2),
            pl.BlockSpec((1, SSD_DINNER), const2),
            pl.BlockSpec((SSD_DINNER, D_MODEL), const2),
        ]
        args += [gz3d, dskip_e, ssd_norm, w_ssd]
        out_specs = pl.BlockSpec((nseq, t, D_MODEL), lambda bb, cc: (bb, cc, 0))
        out_shape = jax.ShapeDtypeStruct((b, l, D_MODEL), BF16)
        scratch.append(pltpu.VMEM((nseq * t, SSD_DINNER), BF16))
    else:
        assert b == 1 and nseq == 1
        out_specs = pl.BlockSpec((SSD_STATE, SSD_DINNER), const2)
        out_shape = jax.ShapeDtypeStruct((SSD_STATE, SSD_DINNER), F32)
    return pl.pallas_call(
        functools.partial(_ssd_kernel, t=t, nseq=nseq, emit_out=emit_out),
        grid=(b // nseq, nc),
        in_specs=in_specs,
        out_specs=out_specs,
        out_shape=out_shape,
        scratch_shapes=scratch,
        compiler_params=_params("parallel", "arbitrary"),
        name="ssd" if emit_out else "ssd_meta",
    )(*args)


def _tail_kernel(x_ref, ba_ref, bb_ref, gates_ref, wout_ref, npost_ref, nfpre_ref, wg_ref, wu_ref,
                 wd_ref, nfpost_ref, o_ref):
    ga = gates_ref[:, :D_MODEL].astype(F32)
    gb = gates_ref[:, D_MODEL:].astype(F32)
    merged = ga * ba_ref[...].astype(F32) + gb * bb_ref[...].astype(F32)
    mix = _dot(merged.astype(BF16), wout_ref[...])
    h1 = x_ref[...] + _rms(mix, npost_ref[...])
    u = _rms(h1, nfpre_ref[...]).astype(BF16)
    act = (_silu(_dot(u, wg_ref[...])) * _dot(u, wu_ref[...])).astype(BF16)
    f = _dot(act, wd_ref[...])
    o_ref[...] = h1 + _rms(f, nfpost_ref[...])


def _tail(x2d, ba, bb, gates, w_out, n_post, nf_pre, w_gate, w_up, w_down, nf_post, tm):
    m = x2d.shape[0]
    assert m % tm == 0
    row = lambda i: (i, 0)
    const = lambda i: (0, 0)
    resident = functools.partial(pl.BlockSpec, index_map=const, pipeline_mode=pl.Buffered(1))
    return pl.pallas_call(
        _tail_kernel,
        grid=(m // tm,),
        in_specs=[
            pl.BlockSpec((tm, D_MODEL), row),
            pl.BlockSpec((tm, D_MODEL), row),
            pl.BlockSpec((tm, D_MODEL), row),
            pl.BlockSpec((tm, 2 * D_MODEL), row),
            resident((D_MODEL, D_MODEL)),
            pl.BlockSpec((1, D_MODEL), const),
            pl.BlockSpec((1, D_MODEL), const),
            resident((D_MODEL, D_FF)),
            resident((D_MODEL, D_FF)),
            resident((D_FF, D_MODEL)),
            pl.BlockSpec((1, D_MODEL), const),
        ],
        out_specs=pl.BlockSpec((tm, D_MODEL), row),
        out_shape=jax.ShapeDtypeStruct((m, D_MODEL), F32),
        compiler_params=_params("parallel"),
        name="tail",
    )(x2d, ba, bb, gates, w_out, n_post, nf_pre, w_gate, w_up, w_down, nf_post)


PROJ_TM = 512
RET_T = 256
SSD_T = 128
SSD_NSEQ = 4
TAIL_TM = 512


def _input_projection(x2d, meta, gain, w_dt, weights, cos, sin, conv_w, conv_b, tm):
    rows_per_seq = cos.shape[0]
    interleave = meta is not None
    hn, dt, *hnp = _prenorm(x2d, gain, w_dt, tm, rows_per_seq, meta)
    qk, v, gz, gates = _main_projection(hn, weights["main"], cos, sin, tm)
    if interleave:
        xbc = _linear(hnp[0], weights["xbc"], "conv_interleaved", tm, rows_per_seq,
                      extra=(conv_w, conv_b))
    else:
        xbc = _linear(hn, weights["xbc"], "conv", tm, rows_per_seq,
                      extra=(jnp.zeros((HALO, D_MODEL), BF16), conv_w, conv_b))
    return dt, qk, v, gz, xbc, gates


def kernel(x, meta_tokens, norm_mix_pre, w_in, conv_w, conv_b, dt_bias, a_log, d_skip, ssd_norm,
           w_ret_branch, w_ssd_branch, w_out, norm_mix_post, norm_ffn_pre, w_gate, w_up, w_down,
           norm_ffn_post):
    b, seq, _ = x.shape
    assert norm_mix_pre.shape[0] == 1, "single-layer block"
    assert meta_tokens.shape[0] == N_META == HALO

    w_in0 = w_in[0]
    weights = {
        "main": jnp.concatenate([w_in0[:, COL_Q:COL_XBC], w_in0[:, COL_GATES:]], axis=1).astype(BF16),
        "xbc": w_in0[:, COL_XBC:COL_DT].astype(BF16),
    }
    w_dt1 = w_in0[:, COL_DT:COL_GATES]
    w_dt = jnp.pad(jnp.concatenate([w_dt1, w_dt1], axis=1), ((0, 0), (0, 128 - DT_W))).astype(BF16)
    w_ret = w_ret_branch[0].astype(BF16)
    w_ssd = w_ssd_branch[0].astype(BF16)
    expand_mat = jnp.tile(jnp.repeat(jnp.eye(SSD_HEADS, dtype=BF16), SSD_HEADDIM, axis=1), (2, 1))
    dskip_e = jnp.repeat(d_skip[0], SSD_HEADDIM)[None, :]
    gain_pre = norm_mix_pre[0][None, :]
    conv_b2 = conv_b[0][None, :]
    dt_bias2 = jnp.tile(dt_bias[0], 2)[None, :]
    a_log2 = jnp.tile(a_log[0], 2)[None, :]
    ssd_norm2 = ssd_norm[0][None, :]

    half = RET_DK // 2
    inv = ROPE_BASE ** (-jnp.arange(half, dtype=F32) / half)
    ang = jnp.arange(N_META + seq, dtype=F32)[:, None] * inv[None, :]
    cos, sin = jnp.cos(ang), jnp.sin(ang)

    meta = meta_tokens.astype(F32)
    dt_m, qk_m, v_m, _, xbc_m, _ = _input_projection(
        meta, None, gain_pre, w_dt, weights, cos[:N_META], sin[:N_META], conv_w[0], conv_b2, N_META)
    ret_st = _retention(qk_m[None], v_m[None], None, jnp.zeros((RET_HEADS, RET_DK, RET_DV), F32),
                        None, N_META, False)
    ssd_st = _ssd(xbc_m[None], None, dt_m[None], jnp.zeros((SSD_STATE, SSD_DINNER), F32), dt_bias2,
                  a_log2, expand_mat, None, None, None, N_META, 1, False)

    x2d = x.reshape(b * seq, D_MODEL)
    dt, qk, v, gz, xbc, gates = _input_projection(
        x2d, meta, gain_pre, w_dt, weights, cos[N_META:], sin[N_META:], conv_w[0], conv_b2, PROJ_TM)
    as3d = lambda a: a.reshape(b, seq, a.shape[-1])
    branch_a = _retention(as3d(qk), as3d(v), as3d(gz), ret_st, w_ret, RET_T, True)
    branch_b = _ssd(as3d(xbc), as3d(gz), as3d(dt), ssd_st, dt_bias2, a_log2, expand_mat, dskip_e,
                    ssd_norm2, w_ssd, SSD_T, SSD_NSEQ, True)
    out = _tail(x2d, branch_a.reshape(b * seq, D_MODEL), branch_b.reshape(b * seq, D_MODEL), gates,
                w_out[0].astype(BF16), norm_mix_post[0][None, :], norm_ffn_pre[0][None, :],
                w_gate[0].astype(BF16), w_up[0].astype(BF16), w_down[0].astype(BF16),
                norm_ffn_post[0][None, :], TAIL_TM)
    return out.reshape(b, seq, D_MODEL)
```

```python
import functools

import numpy as np
import jax
import jax.numpy as jnp
from jax import lax
from jax.experimental import pallas as pl
from jax.experimental.pallas import tpu as pltpu

F32 = jnp.float32
BF16 = jnp.bfloat16

D_MODEL = 1024
N_META = 16
EPS = 1e-6
RET_HEADS = 4
RET_DK = 256
RET_DV = 512
RET_QK = RET_HEADS * RET_DK
RET_V = RET_HEADS * RET_DV
ROPE_BASE = 10000.0
SSD_DINNER = 2 * D_MODEL
SSD_HEADDIM = 64
SSD_HEADS = SSD_DINNER // SSD_HEADDIM
SSD_GROUPS = 8
SSD_HPG = SSD_HEADS // SSD_GROUPS
SSD_STATE = 128
SSD_CONV = 4
SSD_BC = SSD_GROUPS * SSD_STATE
SSD_XBC = SSD_DINNER + 2 * SSD_BC
SSD_NORM_GROUP = SSD_DINNER // SSD_GROUPS
D_FF = 2816
LOG2E = 1.4426950408889634

COL_Q = 0
COL_K = RET_QK
COL_V = 2 * RET_QK
COL_G = COL_V + RET_V
COL_Z = COL_G + RET_V
COL_XBC = COL_Z + SSD_DINNER
COL_DT = COL_XBC + SSD_XBC
COL_GATES = COL_DT + SSD_HEADS
PROJ_TN = 1024
HALO = 16
LEAD = 32
SEG_PAD = 8
DT_W = 2 * SSD_HEADS

VMEM_LIMIT = 56 * 1024 * 1024


def _dot(a, b):
    return jnp.dot(a, b, preferred_element_type=F32)


def _dot_nt(a, b):
    return lax.dot_general(a, b, (((1,), (1,)), ((), ())), preferred_element_type=F32)


def _dot_tn(a, b):
    return lax.dot_general(a, b, (((0,), (0,)), ((), ())), preferred_element_type=F32)


def _sigmoid(x):
    return 0.5 * jnp.tanh(0.5 * x) + 0.5


def _silu(x):
    h = 0.5 * x
    return h * jnp.tanh(h) + h


def _split2(v):
    hi = v.astype(BF16)
    lo = (v - hi.astype(F32)).astype(BF16)
    return hi, lo


def _split3(v):
    hi = v.astype(BF16)
    r = v - hi.astype(F32)
    mid = r.astype(BF16)
    lo = (r - mid.astype(F32)).astype(BF16)
    return hi, mid, lo


def _rms(v, gain):
    return v * lax.rsqrt(jnp.mean(v * v, axis=-1, keepdims=True) + EPS) * gain


def _params(*sem):
    return pltpu.CompilerParams(dimension_semantics=sem, vmem_limit_bytes=VMEM_LIMIT)


def _lane_blocks(ref_or_val, rows, width):
    return [ref_or_val[rows, c * 128:(c + 1) * 128] for c in range(width // 128)]


def _norm_kernel(*refs, tm, seq_tiles, permute):
    if permute:
        x_ref, g_ref, prev_ref, meta_ref, hnp_ref, stg_ref = refs
    else:
        x_ref, g_ref, hn_ref = refs
    hn_f = _rms(x_ref[...], g_ref[...])
    if not permute:
        hn_ref[...] = hn_f.astype(BF16)
    else:
        seg = tm // 8
        groups = seg // 8
        back = LEAD // 8
        first = (pl.program_id(0) % seq_tiles) == 0
        left_f = _rms(jnp.where(first, meta_ref[N_META - 8:, :], prev_ref[...]), g_ref[...])
        for i in range(8):
            src = left_f if i == 0 else hn_f[i * seg - 8:i * seg, :]
            for c, blk in enumerate(_lane_blocks(src, slice(8 - back, 8), D_MODEL)):
                stg_ref[c, pl.ds(i, back, stride=8), :] = blk
        for m in range(tm // 8):
            i, j0 = m // groups, 8 * (m % groups)
            for c, blk in enumerate(_lane_blocks(hn_f, slice(8 * m, 8 * m + 8), D_MODEL)):
                stg_ref[c, pl.ds(LEAD + 8 * j0 + i, 8, stride=8), :] = blk
        hnp_ref[...] = jnp.concatenate(
            [stg_ref[c] for c in range(D_MODEL // 128)], axis=1).astype(BF16)


def _prenorm(x2d, gain, tm, rows_per_seq, meta=None):
    m = x2d.shape[0]
    permute = meta is not None
    assert m % tm == 0 and rows_per_seq % tm == 0
    row = lambda i: (i, 0)
    const = lambda i: (0, 0)
    in_specs = [pl.BlockSpec((tm, D_MODEL), row), pl.BlockSpec((1, D_MODEL), const)]
    args = [x2d, gain]
    out_specs = pl.BlockSpec((tm, D_MODEL), row)
    out_shape = jax.ShapeDtypeStruct((m, D_MODEL), BF16)
    scratch = []
    if permute:
        assert tm % 64 == 0
        in_specs += [pl.BlockSpec((8, D_MODEL), lambda i: (jnp.maximum(i * (tm // 8) - 1, 0), 0)),
                     pl.BlockSpec((N_META, D_MODEL), const)]
        args += [x2d, meta]
        out_specs = pl.BlockSpec((tm + LEAD, D_MODEL), row)
        out_shape = jax.ShapeDtypeStruct((m // tm * (tm + LEAD), D_MODEL), BF16)
        scratch.append(pltpu.VMEM((D_MODEL // 128, tm + LEAD, 128), F32))
    return pl.pallas_call(
        functools.partial(_norm_kernel, tm=tm, seq_tiles=rows_per_seq // tm, permute=permute),
        grid=(m // tm,),
        in_specs=in_specs,
        out_specs=out_specs,
        out_shape=out_shape,
        scratch_shapes=scratch,
        compiler_params=_params("parallel"),
        name="prenorm",
    )(*args)


def _rotary(acc, cos, sin, scale):
    half = RET_DK // 2
    cos = cos * scale
    sin = sin * scale
    parts = []
    for h in range(RET_HEADS):
        t1 = acc[:, h * RET_DK:h * RET_DK + half]
        t2 = acc[:, h * RET_DK + half:(h + 1) * RET_DK]
        parts += [t1 * cos - t2 * sin, t1 * sin + t2 * cos]
    return jnp.concatenate(parts, axis=1)


MAIN_PLAN = ((("rotary_q", 0, 0), ("rotary_k", 0, 1))
             + tuple(("identity", 1, j) for j in range(RET_V // PROJ_TN))
             + tuple(("silu", 2, j) for j in range((RET_V + SSD_DINNER) // PROJ_TN))
             + tuple(("sigmoid", 3, j) for j in range(2 * D_MODEL // PROJ_TN)))
MAIN_WIDTHS = (2 * RET_QK, RET_V, RET_V + SSD_DINNER, 2 * D_MODEL)


def _main_kernel(x_ref, g_ref, wdt_ref, w_ref, cos_ref, sin_ref, *refs):
    *o_refs, dt_ref, hn_ref = refs
    hn_ref[...] = _rms(x_ref[...], g_ref[...]).astype(BF16)
    dt_ref[...] = _dot(hn_ref[...], wdt_ref[...])[:, :DT_W]
    for jt, (kind, oi, oc) in enumerate(MAIN_PLAN):
        acc = _dot(hn_ref[...], w_ref[:, jt * PROJ_TN:(jt + 1) * PROJ_TN])
        if kind == "rotary_q":
            acc = _rotary(acc, cos_ref[...], sin_ref[...], 1.0)
        elif kind == "rotary_k":
            acc = _rotary(acc, cos_ref[...], sin_ref[...], RET_DK ** -0.5)
        elif kind == "silu":
            acc = _silu(acc)
        elif kind == "sigmoid":
            acc = _sigmoid(acc)
        else:
            assert kind == "identity"
        o_refs[oi][:, oc * PROJ_TN:(oc + 1) * PROJ_TN] = acc.astype(BF16)


def _main_projection(x2d, gain, w_dt, w, cos, sin, tm):
    m = x2d.shape[0]
    rows_per_seq = cos.shape[0]
    assert m % tm == 0 and rows_per_seq % tm == 0 and w.shape[1] == sum(MAIN_WIDTHS)
    seq_tiles = rows_per_seq // tm
    row = lambda i: (i, 0)
    const = lambda i: (0, 0)
    pos = pl.BlockSpec((tm, RET_DK // 2), lambda i: (i % seq_tiles, 0))
    *outs, dt = pl.pallas_call(
        _main_kernel,
        grid=(m // tm,),
        in_specs=[pl.BlockSpec((tm, D_MODEL), row), pl.BlockSpec((1, D_MODEL), const),
                  pl.BlockSpec((D_MODEL, 128), const),
                  pl.BlockSpec(w.shape, const, pipeline_mode=pl.Buffered(1)), pos, pos],
        out_specs=[pl.BlockSpec((tm, n), row) for n in MAIN_WIDTHS] + [pl.BlockSpec((tm, DT_W), row)],
        out_shape=([jax.ShapeDtypeStruct((m, n), BF16) for n in MAIN_WIDTHS]
                   + [jax.ShapeDtypeStruct((m, DT_W), F32)]),
        scratch_shapes=[pltpu.VMEM((tm, D_MODEL), BF16)],
        compiler_params=_params("parallel"),
        name="proj_main",
    )(x2d, gain, w_dt, w, cos, sin)
    return outs, dt


def _lin_kernel(*refs, kind, tm, n, seq_tiles):
    ncol = n // PROJ_TN
    cols = lambda jt: slice(jt * PROJ_TN, (jt + 1) * PROJ_TN)
    if kind == "conv":
        hn_ref, w_ref, halo_ref, meta_ref, cw_ref, cb_ref, o_ref = refs
        first = (pl.program_id(0) % seq_tiles) == 0
        lhs = jnp.concatenate([jnp.where(first, meta_ref[...], halo_ref[...]), hn_ref[...]], axis=0)
        rows = tm + 8
        for jt in range(ncol):
            acc = _dot(lhs, w_ref[:, cols(jt)])
            full = acc[HALO - 8:, :]
            out = cb_ref[:, cols(jt)] + cw_ref[SSD_CONV - 1:SSD_CONV, cols(jt)] * full[8:, :]
            for k in range(SSD_CONV - 1):
                back = SSD_CONV - 1 - k
                win = pltpu.roll(full, rows - (8 - back), axis=0)[:tm, :]
                out = out + cw_ref[k:k + 1, cols(jt)] * win
            o_ref[:, cols(jt)] = _silu(out).astype(BF16)
    elif kind == "conv_interleaved":
        hnp_ref, w_ref, cw_ref, cb_ref, o_ref, stg_ref, acc_ref = refs
        seg = tm // 8
        dyn0 = jnp.minimum(pl.program_id(0), 0)
        for jt in range(ncol):
            slot = jt % 2 + dyn0
            acc_ref[slot] = _dot(hnp_ref[...], w_ref[:, cols(jt)])
            out = cb_ref[:, cols(jt)]
            for back in range(SSD_CONV):
                k = SSD_CONV - 1 - back
                out = out + cw_ref[k:k + 1, cols(jt)] * acc_ref[slot, LEAD - 8 * back:LEAD - 8 * back + tm, :]
            act = _silu(out)
            for j in range(seg):
                for c, blk in enumerate(_lane_blocks(act, slice(8 * j, 8 * j + 8), PROJ_TN)):
                    stg_ref[jt % 2, c, pl.ds(j, 8, stride=seg + SEG_PAD), :] = blk
            for i in range(8):
                rows = slice((seg + SEG_PAD) * i, (seg + SEG_PAD) * i + seg)
                o_ref[seg * i:seg * (i + 1), cols(jt)] = jnp.concatenate(
                    [stg_ref[jt % 2, c, rows, :] for c in range(PROJ_TN // 128)], axis=1).astype(BF16)
    else:
        raise ValueError(kind)


def _linear(hn, w, kind, tm, rows_per_seq, extra=()):
    interleaved = kind == "conv_interleaved"
    rows_in = tm + LEAD if interleaved else tm
    assert hn.shape[0] % rows_in == 0
    m = hn.shape[0] // rows_in * tm
    n = w.shape[1]
    assert rows_per_seq % tm == 0 and n % PROJ_TN == 0 and tm % HALO == 0
    seq_tiles = rows_per_seq // tm
    const = lambda i: (0, 0)
    resident = functools.partial(pl.BlockSpec, index_map=const, pipeline_mode=pl.Buffered(1))
    in_specs = [pl.BlockSpec((rows_in, D_MODEL), lambda i: (i, 0)), resident((D_MODEL, n))]
    args = [hn, w]
    if kind == "conv":
        meta_hn, conv_w, conv_b = extra
        in_specs += [
            pl.BlockSpec((HALO, D_MODEL), lambda i: (jnp.maximum(i * (tm // HALO) - 1, 0), 0)),
            pl.BlockSpec((HALO, D_MODEL), const),
            pl.BlockSpec((SSD_CONV, n), const),
            pl.BlockSpec((1, n), const),
        ]
        args += [hn, meta_hn, conv_w, conv_b]
    elif interleaved:
        in_specs += [pl.BlockSpec((SSD_CONV, n), const), pl.BlockSpec((1, n), const)]
        args += list(extra)
    scratch = []
    if interleaved:
        assert tm % 64 == 0
        scratch.append(pltpu.VMEM((2, PROJ_TN // 128, 8 * (tm // 8 + SEG_PAD), 128), F32))
        scratch.append(pltpu.VMEM((2, tm + LEAD, PROJ_TN), F32))
    return pl.pallas_call(
        functools.partial(_lin_kernel, kind=kind, tm=tm, n=n, seq_tiles=seq_tiles),
        grid=(m // tm,),
        in_specs=in_specs,
        out_specs=pl.BlockSpec((tm, n), lambda i: (i, 0)),
        out_shape=jax.ShapeDtypeStruct((m, n), BF16),
        scratch_shapes=scratch,
        compiler_params=_params("parallel"),
        name="proj_" + kind,
    )(*args)


def _ret_tables(t):
    log_g = np.log1p(-np.exp2(-5.0 - np.arange(RET_HEADS, dtype=np.float64)))
    idx = np.arange(t, dtype=np.float64)
    diff = idx[:, None] - idx[None, :]
    dmask = np.where(diff >= 0, np.exp(log_g[:, None, None] * np.maximum(diff, 0.0)), 0.0)
    xi = np.exp(log_g[:, None] * (idx[None, :] + 1.0))
    zeta = np.exp(log_g[:, None] * (t - 1.0 - idx[None, :]))
    g_chunk = np.exp(log_g * t)
    xi = np.broadcast_to(xi[:, :, None], (RET_HEADS, t, RET_DK))
    zeta = np.broadcast_to(zeta[:, :, None], (RET_HEADS, t, RET_DK))
    return (jnp.asarray(dmask, F32), jnp.asarray(xi, F32), jnp.asarray(zeta, F32),
            tuple(float(v) for v in g_chunk))


def _ret_kernel(*refs, g_chunk, emit_out):
    if emit_out:
        (q_ref, k_ref, v_ref, st0_ref, dmask_ref, xi_ref, zeta_ref, g_ref, w_ref,
         o_ref, st_ref, ret_ref) = refs
    else:
        (q_ref, k_ref, v_ref, st0_ref, dmask_ref, xi_ref, zeta_ref, stout_ref, st_ref) = refs
    c = pl.program_id(1)

    @pl.when(c == 0)
    def _():
        st_ref[...] = st0_ref[...]

    for h in range(RET_HEADS):
        qh = q_ref[0, :, h * RET_DK:(h + 1) * RET_DK]
        kh = k_ref[0, :, h * RET_DK:(h + 1) * RET_DK]
        vh = v_ref[0, :, h * RET_DV:(h + 1) * RET_DV]
        s = _dot_nt(qh, kh) * dmask_ref[h]
        qx = (qh.astype(F32) * xi_ref[h]).astype(BF16)
        kz = (kh.astype(F32) * zeta_ref[h]).astype(BF16)
        st = st_ref[h]
        if emit_out:
            out = _dot(s.astype(BF16), vh) + _dot(qx, st.astype(BF16))
            r = out * lax.rsqrt(jnp.mean(out * out, axis=-1, keepdims=True) + EPS)
            gate = g_ref[0, :, h * RET_DV:(h + 1) * RET_DV].astype(F32)
            ret_ref[:, h * RET_DV:(h + 1) * RET_DV] = (r * gate).astype(BF16)
        st_ref[h] = st * g_chunk[h] + _dot_tn(kz, vh)

    if emit_out:
        o_ref[0] = _dot(ret_ref[...], w_ref[...]).astype(BF16)
    else:
        @pl.when(c == pl.num_programs(1) - 1)
        def _():
            stout_ref[...] = st_ref[...]


def _retention(qk3d, v3d, gz3d, st0, w_ret, t, emit_out):
    b, l, _ = qk3d.shape
    assert l % t == 0
    nc = l // t
    dmask, xi, zeta, g_chunk = _ret_tables(t)
    const3 = lambda bb, cc: (0, 0, 0)
    in_specs = [
        pl.BlockSpec((1, t, RET_QK), lambda bb, cc: (bb, cc, 0)),
        pl.BlockSpec((1, t, RET_QK), lambda bb, cc: (bb, cc, 1)),
        pl.BlockSpec((1, t, RET_V), lambda bb, cc: (bb, cc, 0)),
        pl.BlockSpec((RET_HEADS, RET_DK, RET_DV), const3),
        pl.BlockSpec((RET_HEADS, t, t), const3),
        pl.BlockSpec((RET_HEADS, t, RET_DK), const3),
        pl.BlockSpec((RET_HEADS, t, RET_DK), const3),
    ]
    args = [qk3d, qk3d, v3d, st0, dmask, xi, zeta]
    scratch = [pltpu.VMEM((RET_HEADS, RET_DK, RET_DV), F32)]
    if emit_out:
        in_specs += [
            pl.BlockSpec((1, t, RET_V), lambda bb, cc: (bb, cc, 0)),
            pl.BlockSpec((RET_V, D_MODEL), lambda bb, cc: (0, 0)),
        ]
        args += [gz3d, w_ret]
        out_specs = pl.BlockSpec((1, t, D_MODEL), lambda bb, cc: (bb, cc, 0))
        out_shape = jax.ShapeDtypeStruct((b, l, D_MODEL), BF16)
        scratch.append(pltpu.VMEM((t, RET_V), BF16))
    else:
        assert b == 1
        out_specs = pl.BlockSpec((RET_HEADS, RET_DK, RET_DV), const3)
        out_shape = jax.ShapeDtypeStruct((RET_HEADS, RET_DK, RET_DV), F32)
    return pl.pallas_call(
        functools.partial(_ret_kernel, g_chunk=g_chunk, emit_out=emit_out),
        grid=(b, nc),
        in_specs=in_specs,
        out_specs=out_specs,
        out_shape=out_shape,
        scratch_shapes=scratch,
        compiler_params=_params("parallel", "arbitrary"),
        name="retention" if emit_out else "retention_meta",
    )(*args)


def _ssd_kernel(*refs, t, nseq, emit_out):
    if emit_out:
        (xbc_ref, dt_ref, st0_ref, dtb_ref, alog_ref, e_ref,
         z_ref, dskip_ref, norm_ref, w_ref, o_ref, st_ref, y_ref) = refs
    else:
        (xbc_ref, dt_ref, st0_ref, dtb_ref, alog_ref, e_ref, stout_ref, st_ref) = refs
    c = pl.program_id(1)

    @pl.when(c == 0)
    def _():
        for q in range(nseq):
            st_ref[q] = st0_ref[...]

    row = lax.broadcasted_iota(jnp.int32, (t, t), 0)
    col = lax.broadcasted_iota(jnp.int32, (t, t), 1)
    causal = row >= col
    tril = jnp.where(causal, 1.0, 0.0).astype(BF16)
    lane = lax.broadcasted_iota(jnp.int32, (t, DT_W), 1)
    head_of_col = lax.broadcasted_iota(jnp.int32, (t, SSD_NORM_GROUP), 1) // SSD_HEADDIM
    zeros_b = jnp.zeros((t, SSD_STATE), BF16)

    for q in range(nseq):
        dt_raw = dt_ref[q] + dtb_ref[...]
        dt = jnp.maximum(dt_raw, 0.0) + jnp.log1p(jnp.exp(-jnp.abs(dt_raw)))
        da = dt * (-jnp.exp(alog_ref[...]))
        cs = sum(_dot(tril, part) for part in _split3(da))
        ecs = jnp.exp(cs)
        edl = jnp.exp(cs[t - 1:t, :] - cs)
        cs2 = cs * LOG2E
        cs2_t = cs2.T

        def expand(v):
            hi, lo = _split2(v)
            return _dot(jnp.where(lane < SSD_HEADS, hi, lo), e_ref[...])

        xs = xbc_ref[q, :, :SSD_DINNER].astype(F32)
        xdt_f = xs * expand(dt)
        xdt = xdt_f.astype(BF16)
        xdec = (xdt_f * expand(edl)).astype(BF16)
        ecs_e = expand(ecs)

        def b_cols(g):
            return xbc_ref[q, :, SSD_DINNER + g * SSD_STATE:SSD_DINNER + (g + 1) * SSD_STATE]

        for pair in range(SSD_GROUPS // 2):
            if emit_out:
                c2 = xbc_ref[q, :, SSD_DINNER + SSD_BC + 2 * pair * SSD_STATE:
                             SSD_DINNER + SSD_BC + (2 * pair + 2) * SSD_STATE]
                b2 = jnp.concatenate([jnp.concatenate([b_cols(2 * pair), zeros_b], axis=1),
                                      jnp.concatenate([zeros_b, b_cols(2 * pair + 1)], axis=1)], axis=0)
                cb2 = _dot_nt(c2, b2)
            for half in range(2):
                g = 2 * pair + half
                gs = slice(g * SSD_NORM_GROUP, (g + 1) * SSD_NORM_GROUP)
                bg = b_cols(g)
                stg = st_ref[q, :, gs]
                if emit_out:
                    cg = c2[:, half * SSD_STATE:(half + 1) * SSD_STATE]
                    cb = jnp.where(causal, cb2[:, half * t:(half + 1) * t], 0.0)
                    xg = xdt[:, gs]
                    ms, xm = [], []
                    for h in range(SSD_HPG):
                        hh = g * SSD_HPG + h
                        seg = jnp.minimum(cs2[:, hh:hh + 1] - cs2_t[hh:hh + 1, :], 0.0)
                        ms.append((cb * jnp.exp2(seg)).astype(BF16))
                        xm.append(jnp.where(head_of_col == h, xg, jnp.zeros_like(xg)))
                    y = _dot(jnp.concatenate(ms, axis=1), jnp.concatenate(xm, axis=0))
                    y = y + _dot(cg, stg.astype(BF16)) * ecs_e[:, gs]
                    y = y + dskip_ref[:, gs] * xs[:, gs]
                    y = y * z_ref[q, :, gs].astype(F32)
                    y = y * lax.rsqrt(jnp.mean(y * y, axis=-1, keepdims=True) + EPS)
                    y_ref[q * t:(q + 1) * t, gs] = (y * norm_ref[:, gs]).astype(BF16)
                st_ref[q, :, gs] = stg * ecs_e[t - 1:t, gs] + _dot_tn(bg, xdec[:, gs])

    if emit_out:
        out = _dot(y_ref[...], w_ref[...]).astype(BF16)
        for q in range(nseq):
            o_ref[q] = out[q * t:(q + 1) * t, :]
    else:
        @pl.when(c == pl.num_programs(1) - 1)
        def _():
            stout_ref[...] = st_ref[0]


def _ssd(xbc3d, gz3d, dt3d, st0, dt_bias, a_log, expand_mat, dskip_e, ssd_norm, w_ssd, t, nseq, emit_out):
    b, l, _ = xbc3d.shape
    assert l % t == 0 and b % nseq == 0
    nc = l // t
    const2 = lambda bb, cc: (0, 0)
    in_specs = [
        pl.BlockSpec((nseq, t, SSD_XBC), lambda bb, cc: (bb, cc, 0)),
        pl.BlockSpec((nseq, t, DT_W), lambda bb, cc: (bb, cc, 0)),
        pl.BlockSpec((SSD_STATE, SSD_DINNER), const2),
        pl.BlockSpec((1, DT_W), const2),
        pl.BlockSpec((1, DT_W), const2),
        pl.BlockSpec((DT_W, SSD_DINNER), const2),
    ]
    args = [xbc3d, dt3d, st0, dt_bias, a_log, expand_mat]
    scratch = [pltpu.VMEM((nseq, SSD_STATE, SSD_DINNER), F32)]
    if emit_out:
        in_specs += [
            pl.BlockSpec((nseq, t, SSD_DINNER), lambda bb, cc: (bb, cc, RET_V // SSD_DINNER)),
            pl.BlockSpec((1, SSD_DINNER), const2),
            pl.BlockSpec((1, SSD_DINNER), const2),
            pl.BlockSpec((SSD_DINNER, D_MODEL), const2),
        ]
        args += [gz3d, dskip_e, ssd_norm, w_ssd]
        out_specs = pl.BlockSpec((nseq, t, D_MODEL), lambda bb, cc: (bb, cc, 0))
        out_shape = jax.ShapeDtypeStruct((b, l, D_MODEL), BF16)
        scratch.append(pltpu.VMEM((nseq * t, SSD_DINNER), BF16))
    else:
        assert b == 1 and nseq == 1
        out_specs = pl.BlockSpec((SSD_STATE, SSD_DINNER), const2)
        out_shape = jax.ShapeDtypeStruct((SSD_STATE, SSD_DINNER), F32)
    return pl.pallas_call(
        functools.partial(_ssd_kernel, t=t, nseq=nseq, emit_out=emit_out),
        grid=(b // nseq, nc),
        in_specs=in_specs,
        out_specs=out_specs,
        out_shape=out_shape,
        scratch_shapes=scratch,
        compiler_params=_params("parallel", "arbitrary"),
        name="ssd" if emit_out else "ssd_meta",
    )(*args)


def _tail_kernel(x_ref, ba_ref, bb_ref, gates_ref, wout_ref, npost_ref, nfpre_ref, wg_ref, wu_ref,
                 wd_ref, nfpost_ref, o_ref):
    ga = gates_ref[:, :D_MODEL].astype(F32)
    gb = gates_ref[:, D_MODEL:].astype(F32)
    merged = ga * ba_ref[...].astype(F32) + gb * bb_ref[...].astype(F32)
    mix = _dot(merged.astype(BF16), wout_ref[...])
    h1 = x_ref[...] + _rms(mix, npost_ref[...])
    u = _rms(h1, nfpre_ref[...]).astype(BF16)
    act = (_silu(_dot(u, wg_ref[...])) * _dot(u, wu_ref[...])).astype(BF16)
    f = _dot(act, wd_ref[...])
    o_ref[...] = h1 + _rms(f, nfpost_ref[...])


def _tail(x2d, ba, bb, gates, w_out, n_post, nf_pre, w_gate, w_up, w_down, nf_post, tm):
    m = x2d.shape[0]
    assert m % tm == 0
    row = lambda i: (i, 0)
    const = lambda i: (0, 0)
    resident = functools.partial(pl.BlockSpec, index_map=const, pipeline_mode=pl.Buffered(1))
    return pl.pallas_call(
        _tail_kernel,
        grid=(m // tm,),
        in_specs=[
            pl.BlockSpec((tm, D_MODEL), row),
            pl.BlockSpec((tm, D_MODEL), row),
            pl.BlockSpec((tm, D_MODEL), row),
            pl.BlockSpec((tm, 2 * D_MODEL), row),
            resident((D_MODEL, D_MODEL)),
            pl.BlockSpec((1, D_MODEL), const),
            pl.BlockSpec((1, D_MODEL), const),
            resident((D_MODEL, D_FF)),
            resident((D_MODEL, D_FF)),
            resident((D_FF, D_MODEL)),
            pl.BlockSpec((1, D_MODEL), const),
        ],
        out_specs=pl.BlockSpec((tm, D_MODEL), row),
        out_shape=jax.ShapeDtypeStruct((m, D_MODEL), F32),
        compiler_params=_params("parallel"),
        name="tail",
    )(x2d, ba, bb, gates, w_out, n_post, nf_pre, w_gate, w_up, w_down, nf_post)


PROJ_TM = 512
CONV_TM = 1024
RET_T = 256
SSD_T = 128
SSD_NSEQ = 4
TAIL_TM = 512


def _input_projection(x2d, meta, gain, w_dt, weights, cos, sin, conv_w, conv_b, tm, conv_tm):
    rows_per_seq = cos.shape[0]
    interleave = meta is not None
    hn = _prenorm(x2d, gain, conv_tm, rows_per_seq, meta)
    (qk, v, gz, gates), dt = _main_projection(x2d, gain, w_dt, weights["main"], cos, sin, tm)
    if interleave:
        xbc = _linear(hn, weights["xbc"], "conv_interleaved", conv_tm, rows_per_seq,
                      extra=(conv_w, conv_b))
    else:
        xbc = _linear(hn, weights["xbc"], "conv", conv_tm, rows_per_seq,
                      extra=(jnp.zeros((HALO, D_MODEL), BF16), conv_w, conv_b))
    return dt, qk, v, gz, xbc, gates


def kernel(x, meta_tokens, norm_mix_pre, w_in, conv_w, conv_b, dt_bias, a_log, d_skip, ssd_norm,
           w_ret_branch, w_ssd_branch, w_out, norm_mix_post, norm_ffn_pre, w_gate, w_up, w_down,
           norm_ffn_post):
    b, seq, _ = x.shape
    assert norm_mix_pre.shape[0] == 1, "single-layer block"
    assert meta_tokens.shape[0] == N_META == HALO

    w_in0 = w_in[0].astype(BF16)
    weights = {
        "main": jnp.concatenate([w_in0[:, COL_Q:COL_XBC], w_in0[:, COL_GATES:]], axis=1),
        "xbc": w_in0[:, COL_XBC:COL_DT],
    }
    w_dt1 = w_in0[:, COL_DT:COL_GATES]
    w_dt = jnp.pad(jnp.concatenate([w_dt1, w_dt1], axis=1), ((0, 0), (0, 128 - DT_W)))
    w_ret = w_ret_branch[0].astype(BF16)
    w_ssd = w_ssd_branch[0].astype(BF16)
    expand_mat = jnp.tile(jnp.repeat(jnp.eye(SSD_HEADS, dtype=BF16), SSD_HEADDIM, axis=1), (2, 1))
    dskip_e = jnp.repeat(d_skip[0], SSD_HEADDIM)[None, :]
    gain_pre = norm_mix_pre[0][None, :]
    conv_b2 = conv_b[0][None, :]
    dt_bias2 = jnp.tile(dt_bias[0], 2)[None, :]
    a_log2 = jnp.tile(a_log[0], 2)[None, :]
    ssd_norm2 = ssd_norm[0][None, :]

    half = RET_DK // 2
    inv = ROPE_BASE ** (-jnp.arange(half, dtype=F32) / half)
    ang = jnp.arange(N_META + seq, dtype=F32)[:, None] * inv[None, :]
    cos, sin = jnp.cos(ang), jnp.sin(ang)

    meta = meta_tokens.astype(F32)
    dt_m, qk_m, v_m, _, xbc_m, _ = _input_projection(
        meta, None, gain_pre, w_dt, weights, cos[:N_META], sin[:N_META], conv_w[0], conv_b2, N_META,
        N_META)
    ret_st = _retention(qk_m[None], v_m[None], None, jnp.zeros((RET_HEADS, RET_DK, RET_DV), F32),
                        None, N_META, False)
    ssd_st = _ssd(xbc_m[None], None, dt_m[None], jnp.zeros((SSD_STATE, SSD_DINNER), F32), dt_bias2,
                  a_log2, expand_mat, None, None, None, N_META, 1, False)

    x2d = x.reshape(b * seq, D_MODEL)
    dt, qk, v, gz, xbc, gates = _input_projection(
        x2d, meta, gain_pre, w_dt, weights, cos[N_META:], sin[N_META:], conv_w[0], conv_b2, PROJ_TM,
        CONV_TM)
    as3d = lambda a: a.reshape(b, seq, a.shape[-1])
    branch_a = _retention(as3d(qk), as3d(v), as3d(gz), ret_st, w_ret, RET_T, True)
    branch_b = _ssd(as3d(xbc), as3d(gz), as3d(dt), ssd_st, dt_bias2, a_log2, expand_mat, dskip_e,
                    ssd_norm2, w_ssd, SSD_T, SSD_NSEQ, True)
    out = _tail(x2d, branch_a.reshape(b * seq, D_MODEL), branch_b.reshape(b * seq, D_MODEL), gates,
                w_out[0].astype(BF16), norm_mix_post[0][None, :], norm_ffn_pre[0][None, :],
                w_gate[0].astype(BF16), w_up[0].astype(BF16), w_down[0].astype(BF16),
                norm_ffn_post[0][None, :], TAIL_TM)
    return out.reshape(b, seq, D_MODEL)
```

```python
import functools

import numpy as np
import jax
import jax.numpy as jnp
from jax import lax
from jax.experimental import pallas as pl
from jax.experimental.pallas import tpu as pltpu

F32 = jnp.float32
BF16 = jnp.bfloat16

D_MODEL = 1024
N_META = 16
EPS = 1e-6
RET_HEADS = 4
RET_DK = 256
RET_DV = 512
RET_QK = RET_HEADS * RET_DK
RET_V = RET_HEADS * RET_DV
ROPE_BASE = 10000.0
SSD_DINNER = 2 * D_MODEL
SSD_HEADDIM = 64
SSD_HEADS = SSD_DINNER // SSD_HEADDIM
SSD_GROUPS = 8
SSD_HPG = SSD_HEADS // SSD_GROUPS
SSD_STATE = 128
SSD_CONV = 4
SSD_BC = SSD_GROUPS * SSD_STATE
SSD_XBC = SSD_DINNER + 2 * SSD_BC
SSD_NORM_GROUP = SSD_DINNER // SSD_GROUPS
D_FF = 2816
LOG2E = 1.4426950408889634

COL_Q = 0
COL_K = RET_QK
COL_V = 2 * RET_QK
COL_G = COL_V + RET_V
COL_Z = COL_G + RET_V
COL_XBC = COL_Z + SSD_DINNER
COL_DT = COL_XBC + SSD_XBC
COL_GATES = COL_DT + SSD_HEADS
PROJ_TN = 1024
HALO = 16
LEAD = 32
SEG_PAD = 8
DT_W = 2 * SSD_HEADS

VMEM_LIMIT = 56 * 1024 * 1024


def _dot(a, b):
    return jnp.dot(a, b, preferred_element_type=F32)


def _dot_nt(a, b):
    return lax.dot_general(a, b, (((1,), (1,)), ((), ())), preferred_element_type=F32)


def _dot_tn(a, b):
    return lax.dot_general(a, b, (((0,), (0,)), ((), ())), preferred_element_type=F32)


def _sigmoid(x):
    return 0.5 * jnp.tanh(0.5 * x) + 0.5


def _silu(x):
    h = 0.5 * x
    return h * jnp.tanh(h) + h


def _split2(v):
    hi = v.astype(BF16)
    lo = (v - hi.astype(F32)).astype(BF16)
    return hi, lo


def _split3(v):
    hi = v.astype(BF16)
    r = v - hi.astype(F32)
    mid = r.astype(BF16)
    lo = (r - mid.astype(F32)).astype(BF16)
    return hi, mid, lo


def _rms(v, gain):
    return v * lax.rsqrt(jnp.mean(v * v, axis=-1, keepdims=True) + EPS) * gain


def _params(*sem):
    return pltpu.CompilerParams(dimension_semantics=sem, vmem_limit_bytes=VMEM_LIMIT)


def _lane_blocks(ref_or_val, rows, width):
    return [ref_or_val[rows, c * 128:(c + 1) * 128] for c in range(width // 128)]


def _norm_kernel(*refs, tm, seq_tiles, permute):
    if permute:
        x_ref, g_ref, prev_ref, meta_ref, hnp_ref, stg_ref = refs
    else:
        x_ref, g_ref, hn_ref = refs
    hn_f = _rms(x_ref[...], g_ref[...])
    if not permute:
        hn_ref[...] = hn_f.astype(BF16)
    else:
        seg = tm // 8
        groups = seg // 8
        back = LEAD // 8
        first = (pl.program_id(0) % seq_tiles) == 0
        left_f = _rms(jnp.where(first, meta_ref[N_META - 8:, :], prev_ref[...]), g_ref[...])
        for i in range(8):
            src = left_f if i == 0 else hn_f[i * seg - 8:i * seg, :]
            for c, blk in enumerate(_lane_blocks(src, slice(8 - back, 8), D_MODEL)):
                stg_ref[c, pl.ds(i, back, stride=8), :] = blk
        for m in range(tm // 8):
            i, j0 = m // groups, 8 * (m % groups)
            for c, blk in enumerate(_lane_blocks(hn_f, slice(8 * m, 8 * m + 8), D_MODEL)):
                stg_ref[c, pl.ds(LEAD + 8 * j0 + i, 8, stride=8), :] = blk
        hnp_ref[...] = jnp.concatenate(
            [stg_ref[c] for c in range(D_MODEL // 128)], axis=1).astype(BF16)


def _prenorm(x2d, gain, tm, rows_per_seq, meta=None):
    m = x2d.shape[0]
    permute = meta is not None
    assert m % tm == 0 and rows_per_seq % tm == 0
    row = lambda i: (i, 0)
    const = lambda i: (0, 0)
    in_specs = [pl.BlockSpec((tm, D_MODEL), row), pl.BlockSpec((1, D_MODEL), const)]
    args = [x2d, gain]
    out_specs = pl.BlockSpec((tm, D_MODEL), row)
    out_shape = jax.ShapeDtypeStruct((m, D_MODEL), BF16)
    scratch = []
    if permute:
        assert tm % 64 == 0
        in_specs += [pl.BlockSpec((8, D_MODEL), lambda i: (jnp.maximum(i * (tm // 8) - 1, 0), 0)),
                     pl.BlockSpec((N_META, D_MODEL), const)]
        args += [x2d, meta]
        out_specs = pl.BlockSpec((tm + LEAD, D_MODEL), row)
        out_shape = jax.ShapeDtypeStruct((m // tm * (tm + LEAD), D_MODEL), BF16)
        scratch.append(pltpu.VMEM((D_MODEL // 128, tm + LEAD, 128), F32))
    return pl.pallas_call(
        functools.partial(_norm_kernel, tm=tm, seq_tiles=rows_per_seq // tm, permute=permute),
        grid=(m // tm,),
        in_specs=in_specs,
        out_specs=out_specs,
        out_shape=out_shape,
        scratch_shapes=scratch,
        compiler_params=_params("parallel"),
        name="prenorm",
    )(*args)


def _rotary(acc, cos, sin, scale):
    half = RET_DK // 2
    cos = cos * scale
    sin = sin * scale
    parts = []
    for h in range(RET_HEADS):
        t1 = acc[:, h * RET_DK:h * RET_DK + half]
        t2 = acc[:, h * RET_DK + half:(h + 1) * RET_DK]
        parts += [t1 * cos - t2 * sin, t1 * sin + t2 * cos]
    return jnp.concatenate(parts, axis=1)


MAIN_PLAN = ((("rotary_q", 0, 0), ("rotary_k", 0, 1))
             + tuple(("identity", 1, j) for j in range(RET_V // PROJ_TN))
             + tuple(("silu", 2, j) for j in range((RET_V + SSD_DINNER) // PROJ_TN))
             + tuple(("sigmoid", 3, j) for j in range(2 * D_MODEL // PROJ_TN)))
MAIN_WIDTHS = (2 * RET_QK, RET_V, RET_V + SSD_DINNER, 2 * D_MODEL)


def _main_kernel(x_ref, g_ref, wdt_ref, w_ref, cos_ref, sin_ref, *refs):
    *o_refs, dt_ref, hn_ref = refs
    hn_ref[...] = _rms(x_ref[...], g_ref[...]).astype(BF16)
    dt_ref[...] = _dot(hn_ref[...], wdt_ref[...])[:, :DT_W]
    for jt, (kind, oi, oc) in enumerate(MAIN_PLAN):
        acc = _dot(hn_ref[...], w_ref[:, jt * PROJ_TN:(jt + 1) * PROJ_TN])
        if kind == "rotary_q":
            acc = _rotary(acc, cos_ref[...], sin_ref[...], 1.0)
        elif kind == "rotary_k":
            acc = _rotary(acc, cos_ref[...], sin_ref[...], RET_DK ** -0.5)
        elif kind == "silu":
            acc = _silu(acc)
        elif kind == "sigmoid":
            acc = _sigmoid(acc)
        else:
            assert kind == "identity"
        o_refs[oi][:, oc * PROJ_TN:(oc + 1) * PROJ_TN] = acc.astype(BF16)


def _main_projection(x2d, gain, w_dt, w, cos, sin, tm):
    m = x2d.shape[0]
    rows_per_seq = cos.shape[0]
    assert m % tm == 0 and rows_per_seq % tm == 0 and w.shape[1] == sum(MAIN_WIDTHS)
    seq_tiles = rows_per_seq // tm
    row = lambda i: (i, 0)
    const = lambda i: (0, 0)
    pos = pl.BlockSpec((tm, RET_DK // 2), lambda i: (i % seq_tiles, 0))
    *outs, dt = pl.pallas_call(
        _main_kernel,
        grid=(m // tm,),
        in_specs=[pl.BlockSpec((tm, D_MODEL), row), pl.BlockSpec((1, D_MODEL), const),
                  pl.BlockSpec((D_MODEL, 128), const),
                  pl.BlockSpec(w.shape, const, pipeline_mode=pl.Buffered(1)), pos, pos],
        out_specs=[pl.BlockSpec((tm, n), row) for n in MAIN_WIDTHS] + [pl.BlockSpec((tm, DT_W), row)],
        out_shape=([jax.ShapeDtypeStruct((m, n), BF16) for n in MAIN_WIDTHS]
                   + [jax.ShapeDtypeStruct((m, DT_W), F32)]),
        scratch_shapes=[pltpu.VMEM((tm, D_MODEL), BF16)],
        compiler_params=_params("parallel"),
        name="proj_main",
    )(x2d, gain, w_dt, w, cos, sin)
    return outs, dt


def _lin_kernel(*refs, kind, tm, n, seq_tiles):
    ncol = n // PROJ_TN
    cols = lambda jt: slice(jt * PROJ_TN, (jt + 1) * PROJ_TN)
    if kind == "conv":
        hn_ref, w_ref, halo_ref, meta_ref, cw_ref, cb_ref, o_ref = refs
        first = (pl.program_id(0) % seq_tiles) == 0
        lhs = jnp.concatenate([jnp.where(first, meta_ref[...], halo_ref[...]), hn_ref[...]], axis=0)
        rows = tm + 8
        for jt in range(ncol):
            acc = _dot(lhs, w_ref[:, cols(jt)])
            full = acc[HALO - 8:, :]
            out = cb_ref[:, cols(jt)] + cw_ref[SSD_CONV - 1:SSD_CONV, cols(jt)] * full[8:, :]
            for k in range(SSD_CONV - 1):
                back = SSD_CONV - 1 - k
                win = pltpu.roll(full, rows - (8 - back), axis=0)[:tm, :]
                out = out + cw_ref[k:k + 1, cols(jt)] * win
            o_ref[:, cols(jt)] = _silu(out).astype(BF16)
    elif kind == "conv_interleaved":
        hnp_ref, w_ref, cw_ref, cb_ref, o_ref, stg_ref, acc_ref = refs
        seg = tm // 8
        dyn0 = jnp.minimum(pl.program_id(0), 0)
        for jt in range(ncol):
            slot = jt % 2 + dyn0
            acc_ref[slot] = _dot(hnp_ref[...], w_ref[:, cols(jt)])
            out = cb_ref[:, cols(jt)]
            for back in range(SSD_CONV):
                k = SSD_CONV - 1 - back
                out = out + cw_ref[k:k + 1, cols(jt)] * acc_ref[slot, LEAD - 8 * back:LEAD - 8 * back + tm, :]
            act = _silu(out)
            for j in range(seg):
                for c, blk in enumerate(_lane_blocks(act, slice(8 * j, 8 * j + 8), PROJ_TN)):
                    stg_ref[jt % 2, c, pl.ds(j, 8, stride=seg + SEG_PAD), :] = blk
            for i in range(8):
                rows = slice((seg + SEG_PAD) * i, (seg + SEG_PAD) * i + seg)
                o_ref[seg * i:seg * (i + 1), cols(jt)] = jnp.concatenate(
                    [stg_ref[jt % 2, c, rows, :] for c in range(PROJ_TN // 128)], axis=1).astype(BF16)
    else:
        raise ValueError(kind)


def _linear(hn, w, kind, tm, rows_per_seq, extra=()):
    interleaved = kind == "conv_interleaved"
    rows_in = tm + LEAD if interleaved else tm
    assert hn.shape[0] % rows_in == 0
    m = hn.shape[0] // rows_in * tm
    n = w.shape[1]
    assert rows_per_seq % tm == 0 and n % PROJ_TN == 0 and tm % HALO == 0
    seq_tiles = rows_per_seq // tm
    const = lambda i: (0, 0)
    resident = functools.partial(pl.BlockSpec, index_map=const, pipeline_mode=pl.Buffered(1))
    in_specs = [pl.BlockSpec((rows_in, D_MODEL), lambda i: (i, 0)), resident((D_MODEL, n))]
    args = [hn, w]
    if kind == "conv":
        meta_hn, conv_w, conv_b = extra
        in_specs += [
            pl.BlockSpec((HALO, D_MODEL), lambda i: (jnp.maximum(i * (tm // HALO) - 1, 0), 0)),
            pl.BlockSpec((HALO, D_MODEL), const),
            pl.BlockSpec((SSD_CONV, n), const),
            pl.BlockSpec((1, n), const),
        ]
        args += [hn, meta_hn, conv_w, conv_b]
    elif interleaved:
        in_specs += [pl.BlockSpec((SSD_CONV, n), const), pl.BlockSpec((1, n), const)]
        args += list(extra)
    scratch = []
    if interleaved:
        assert tm % 64 == 0
        scratch.append(pltpu.VMEM((2, PROJ_TN // 128, 8 * (tm // 8 + SEG_PAD), 128), F32))
        scratch.append(pltpu.VMEM((2, tm + LEAD, PROJ_TN), F32))
    return pl.pallas_call(
        functools.partial(_lin_kernel, kind=kind, tm=tm, n=n, seq_tiles=seq_tiles),
        grid=(m // tm,),
        in_specs=in_specs,
        out_specs=pl.BlockSpec((tm, n), lambda i: (i, 0)),
        out_shape=jax.ShapeDtypeStruct((m, n), BF16),
        scratch_shapes=scratch,
        compiler_params=_params("parallel"),
        name="proj_" + kind,
    )(*args)


def _ret_tables(t):
    log_g = np.log1p(-np.exp2(-5.0 - np.arange(RET_HEADS, dtype=np.float64)))
    idx = np.arange(t, dtype=np.float64)
    diff = idx[:, None] - idx[None, :]
    dmask = np.where(diff >= 0, np.exp(log_g[:, None, None] * np.maximum(diff, 0.0)), 0.0)
    xi = np.exp(log_g[:, None] * (idx[None, :] + 1.0))
    zeta = np.exp(log_g[:, None] * (t - 1.0 - idx[None, :]))
    g_chunk = np.exp(log_g * t)
    xi = np.broadcast_to(xi[:, :, None], (RET_HEADS, t, RET_DK))
    zeta = np.broadcast_to(zeta[:, :, None], (RET_HEADS, t, RET_DK))
    return (jnp.asarray(dmask, F32), jnp.asarray(xi, F32), jnp.asarray(zeta, F32),
            tuple(float(v) for v in g_chunk))


def _ret_kernel(*refs, g_chunk, emit_out):
    if emit_out:
        (q_ref, k_ref, v_ref, st0_ref, dmask_ref, xi_ref, zeta_ref, g_ref, w_ref,
         o_ref, st_ref, ret_ref) = refs
    else:
        (q_ref, k_ref, v_ref, st0_ref, dmask_ref, xi_ref, zeta_ref, stout_ref, st_ref) = refs
    c = pl.program_id(1)

    @pl.when(c == 0)
    def _():
        st_ref[...] = st0_ref[...]

    for h in range(RET_HEADS):
        qh = q_ref[0, :, h * RET_DK:(h + 1) * RET_DK]
        kh = k_ref[0, :, h * RET_DK:(h + 1) * RET_DK]
        vh = v_ref[0, :, h * RET_DV:(h + 1) * RET_DV]
        s = _dot_nt(qh, kh) * dmask_ref[h]
        qx = (qh.astype(F32) * xi_ref[h]).astype(BF16)
        kz = (kh.astype(F32) * zeta_ref[h]).astype(BF16)
        st = st_ref[h]
        if emit_out:
            out = _dot(s.astype(BF16), vh) + _dot(qx, st.astype(BF16))
            r = out * lax.rsqrt(jnp.mean(out * out, axis=-1, keepdims=True) + EPS)
            gate = g_ref[0, :, h * RET_DV:(h + 1) * RET_DV].astype(F32)
            ret_ref[:, h * RET_DV:(h + 1) * RET_DV] = (r * gate).astype(BF16)
        st_ref[h] = st * g_chunk[h] + _dot_tn(kz, vh)

    if emit_out:
        o_ref[0] = _dot(ret_ref[...], w_ref[...]).astype(BF16)
    else:
        @pl.when(c == pl.num_programs(1) - 1)
        def _():
            stout_ref[...] = st_ref[...]


def _retention(qk3d, v3d, gz3d, st0, w_ret, t, emit_out):
    b, l, _ = qk3d.shape
    assert l % t == 0
    nc = l // t
    dmask, xi, zeta, g_chunk = _ret_tables(t)
    const3 = lambda bb, cc: (0, 0, 0)
    in_specs = [
        pl.BlockSpec((1, t, RET_QK), lambda bb, cc: (bb, cc, 0)),
        pl.BlockSpec((1, t, RET_QK), lambda bb, cc: (bb, cc, 1)),
        pl.BlockSpec((1, t, RET_V), lambda bb, cc: (bb, cc, 0)),
        pl.BlockSpec((RET_HEADS, RET_DK, RET_DV), const3),
        pl.BlockSpec((RET_HEADS, t, t), const3),
        pl.BlockSpec((RET_HEADS, t, RET_DK), const3),
        pl.BlockSpec((RET_HEADS, t, RET_DK), const3),
    ]
    args = [qk3d, qk3d, v3d, st0, dmask, xi, zeta]
    scratch = [pltpu.VMEM((RET_HEADS, RET_DK, RET_DV), F32)]
    if emit_out:
        in_specs += [
            pl.BlockSpec((1, t, RET_V), lambda bb, cc: (bb, cc, 0)),
            pl.BlockSpec((RET_V, D_MODEL), lambda bb, cc: (0, 0)),
        ]
        args += [gz3d, w_ret]
        out_specs = pl.BlockSpec((1, t, D_MODEL), lambda bb, cc: (bb, cc, 0))
        out_shape = jax.ShapeDtypeStruct((b, l, D_MODEL), BF16)
        scratch.append(pltpu.VMEM((t, RET_V), BF16))
    else:
        assert b == 1
        out_specs = pl.BlockSpec((RET_HEADS, RET_DK, RET_DV), const3)
        out_shape = jax.ShapeDtypeStruct((RET_HEADS, RET_DK, RET_DV), F32)
    return pl.pallas_call(
        functools.partial(_ret_kernel, g_chunk=g_chunk, emit_out=emit_out),
        grid=(b, nc),
        in_specs=in_specs,
        out_specs=out_specs,
        out_shape=out_shape,
        scratch_shapes=scratch,
        compiler_params=_params("parallel", "arbitrary"),
        name="retention" if emit_out else "retention_meta",
    )(*args)


def _ssd_kernel(*refs, t, nseq, emit_out):
    if emit_out:
        (xbc_ref, dt_ref, st0_ref, dtb_ref, alog_ref, e_ref,
         z_ref, dskip_ref, norm_ref, w_ref, o_ref, st_ref, y_ref) = refs
    else:
        (xbc_ref, dt_ref, st0_ref, dtb_ref, alog_ref, e_ref, stout_ref, st_ref) = refs
    c = pl.program_id(1)

    @pl.when(c == 0)
    def _():
        for q in range(nseq):
            st_ref[q] = st0_ref[...]

    row = lax.broadcasted_iota(jnp.int32, (t, t), 0)
    col = lax.broadcasted_iota(jnp.int32, (t, t), 1)
    causal = row >= col
    tril = jnp.where(causal, 1.0, 0.0).astype(BF16)
    lane = lax.broadcasted_iota(jnp.int32, (t, DT_W), 1)
    head_of_col = lax.broadcasted_iota(jnp.int32, (t, SSD_NORM_GROUP), 1) // SSD_HEADDIM
    zeros_b = jnp.zeros((t, SSD_STATE), BF16)

    for q in range(nseq):
        dt_raw = dt_ref[q] + dtb_ref[...]
        dt = jnp.maximum(dt_raw, 0.0) + jnp.log1p(jnp.exp(-jnp.abs(dt_raw)))
        da = dt * (-jnp.exp(alog_ref[...]))
        cs = sum(_dot(tril, part) for part in _split3(da))
        ecs = jnp.exp(cs)
        edl = jnp.exp(cs[t - 1:t, :] - cs)
        cs2 = cs * LOG2E
        cs2_t = cs2.T

        def expand(v):
            hi, lo = _split2(v)
            return _dot(jnp.where(lane < SSD_HEADS, hi, lo), e_ref[...])

        xs = xbc_ref[q, :, :SSD_DINNER].astype(F32)
        xdt_f = xs * expand(dt)
        xdt = xdt_f.astype(BF16)
        xdec = (xdt_f * expand(edl)).astype(BF16)
        ecs_e = expand(ecs)

        def b_cols(g):
            return xbc_ref[q, :, SSD_DINNER + g * SSD_STATE:SSD_DINNER + (g + 1) * SSD_STATE]

        for pair in range(SSD_GROUPS // 2):
            if emit_out:
                c2 = xbc_ref[q, :, SSD_DINNER + SSD_BC + 2 * pair * SSD_STATE:
                             SSD_DINNER + SSD_BC + (2 * pair + 2) * SSD_STATE]
                b2 = jnp.concatenate([jnp.concatenate([b_cols(2 * pair), zeros_b], axis=1),
                                      jnp.concatenate([zeros_b, b_cols(2 * pair + 1)], axis=1)], axis=0)
                cb2 = _dot_nt(c2, b2)
            for half in range(2):
                g = 2 * pair + half
                gs = slice(g * SSD_NORM_GROUP, (g + 1) * SSD_NORM_GROUP)
                bg = b_cols(g)
                stg = st_ref[q, :, gs]
                if emit_out:
                    cg = c2[:, half * SSD_STATE:(half + 1) * SSD_STATE]
                    cb = jnp.where(causal, cb2[:, half * t:(half + 1) * t], 0.0)
                    xg = xdt[:, gs]
                    ms, xm = [], []
                    for h in range(SSD_HPG):
                        hh = g * SSD_HPG + h
                        seg = jnp.minimum(cs2[:, hh:hh + 1] - cs2_t[hh:hh + 1, :], 0.0)
                        ms.append((cb * jnp.exp2(seg)).astype(BF16))
                        xm.append(jnp.where(head_of_col == h, xg, jnp.zeros_like(xg)))
                    y = _dot(jnp.concatenate(ms, axis=1), jnp.concatenate(xm, axis=0))
                    y = y + _dot(cg, stg.astype(BF16)) * ecs_e[:, gs]
                    y = y + dskip_ref[:, gs] * xs[:, gs]
                    y = y * z_ref[q, :, gs].astype(F32)
                    y = y * lax.rsqrt(jnp.mean(y * y, axis=-1, keepdims=True) + EPS)
                    y_ref[q * t:(q + 1) * t, gs] = (y * norm_ref[:, gs]).astype(BF16)
                st_ref[q, :, gs] = stg * ecs_e[t - 1:t, gs] + _dot_tn(bg, xdec[:, gs])

    if emit_out:
        out = _dot(y_ref[...], w_ref[...]).astype(BF16)
        for q in range(nseq):
            o_ref[q] = out[q * t:(q + 1) * t, :]
    else:
        @pl.when(c == pl.num_programs(1) - 1)
        def _():
            stout_ref[...] = st_ref[0]


def _ssd(xbc3d, gz3d, dt3d, st0, dt_bias, a_log, expand_mat, dskip_e, ssd_norm, w_ssd, t, nseq, emit_out):
    b, l, _ = xbc3d.shape
    assert l % t == 0 and b % nseq == 0
    nc = l // t
    const2 = lambda bb, cc: (0, 0)
    in_specs = [
        pl.BlockSpec((nseq, t, SSD_XBC), lambda bb, cc: (bb, cc, 0)),
        pl.BlockSpec((nseq, t, DT_W), lambda bb, cc: (bb, cc, 0)),
        pl.BlockSpec((SSD_STATE, SSD_DINNER), const2),
        pl.BlockSpec((1, DT_W), const2),
        pl.BlockSpec((1, DT_W), const2),
        pl.BlockSpec((DT_W, SSD_DINNER), const2),
    ]
    args = [xbc3d, dt3d, st0, dt_bias, a_log, expand_mat]
    scratch = [pltpu.VMEM((nseq, SSD_STATE, SSD_DINNER), F32)]
    if emit_out:
        in_specs += [
            pl.BlockSpec((nseq, t, SSD_DINNER), lambda bb, cc: (bb, cc, RET_V // SSD_DINNER)),
            pl.BlockSpec((1, SSD_DINNER), const2),
            pl.BlockSpec((1, SSD_DINNER), const2),
            pl.BlockSpec((SSD_DINNER, D_MODEL), const2),
        ]
        args += [gz3d, dskip_e, ssd_norm, w_ssd]
        out_specs = pl.BlockSpec((nseq, t, D_MODEL), lambda bb, cc: (bb, cc, 0))
        out_shape = jax.ShapeDtypeStruct((b, l, D_MODEL), BF16)
        scratch.append(pltpu.VMEM((nseq * t, SSD_DINNER), BF16))
    else:
        assert b == 1 and nseq == 1
        out_specs = pl.BlockSpec((SSD_STATE, SSD_DINNER), const2)
        out_shape = jax.ShapeDtypeStruct((SSD_STATE, SSD_DINNER), F32)
    return pl.pallas_call(
        functools.partial(_ssd_kernel, t=t, nseq=nseq, emit_out=emit_out),
        grid=(b // nseq, nc),
        in_specs=in_specs,
        out_specs=out_specs,
        out_shape=out_shape,
        scratch_shapes=scratch,
        compiler_params=_params("parallel", "arbitrary"),
        name="ssd" if emit_out else "ssd_meta",
    )(*args)


def _tail_kernel(x_ref, ba_ref, bb_ref, gates_ref, wout_ref, npost_ref, nfpre_ref, wg_ref, wu_ref,
                 wd_ref, nfpost_ref, o_ref):
    rb = x_ref.shape[0] // TAIL_SPLIT
    blocks = [slice(r * rb, (r + 1) * rb) for r in range(TAIL_SPLIT)]
    merged = [(gates_ref[rows, :D_MODEL].astype(F32) * ba_ref[rows, :].astype(F32)
               + gates_ref[rows, D_MODEL:].astype(F32) * bb_ref[rows, :].astype(F32)).astype(BF16)
              for rows in blocks]
    mix = [_dot(m, wout_ref[...]) for m in merged]
    h1 = [x_ref[rows, :] + _rms(mx, npost_ref[...]) for rows, mx in zip(blocks, mix)]
    u = [_rms(h, nfpre_ref[...]).astype(BF16) for h in h1]
    gate = [_dot(v, wg_ref[...]) for v in u]
    up = [_dot(v, wu_ref[...]) for v in u]
    act = [(_silu(g) * p).astype(BF16) for g, p in zip(gate, up)]
    f = [_dot(a, wd_ref[...]) for a in act]
    for rows, h, ff in zip(blocks, h1, f):
        o_ref[rows, :] = h + _rms(ff, nfpost_ref[...])


def _tail(x2d, ba, bb, gates, w_out, n_post, nf_pre, w_gate, w_up, w_down, nf_post, tm):
    m = x2d.shape[0]
    assert m % tm == 0
    row = lambda i: (i, 0)
    const = lambda i: (0, 0)
    resident = functools.partial(pl.BlockSpec, index_map=const, pipeline_mode=pl.Buffered(1))
    return pl.pallas_call(
        _tail_kernel,
        grid=(m // tm,),
        in_specs=[
            pl.BlockSpec((tm, D_MODEL), row),
            pl.BlockSpec((tm, D_MODEL), row),
            pl.BlockSpec((tm, D_MODEL), row),
            pl.BlockSpec((tm, 2 * D_MODEL), row),
            resident((D_MODEL, D_MODEL)),
            pl.BlockSpec((1, D_MODEL), const),
            pl.BlockSpec((1, D_MODEL), const),
            resident((D_MODEL, D_FF)),
            resident((D_MODEL, D_FF)),
            resident((D_FF, D_MODEL)),
            pl.BlockSpec((1, D_MODEL), const),
        ],
        out_specs=pl.BlockSpec((tm, D_MODEL), row),
        out_shape=jax.ShapeDtypeStruct((m, D_MODEL), F32),
        compiler_params=_params("parallel"),
        name="tail",
    )(x2d, ba, bb, gates, w_out, n_post, nf_pre, w_gate, w_up, w_down, nf_post)


PROJ_TM = 512
CONV_TM = 1024
RET_T = 256
SSD_T = 128
SSD_NSEQ = 4
TAIL_TM = 512
TAIL_SPLIT = 2


def _input_projection(x2d, meta, gain, w_dt, weights, cos, sin, conv_w, conv_b, tm, conv_tm):
    rows_per_seq = cos.shape[0]
    interleave = meta is not None
    hn = _prenorm(x2d, gain, conv_tm, rows_per_seq, meta)
    (qk, v, gz, gates), dt = _main_projection(x2d, gain, w_dt, weights["main"], cos, sin, tm)
    if interleave:
        xbc = _linear(hn, weights["xbc"], "conv_interleaved", conv_tm, rows_per_seq,
                      extra=(conv_w, conv_b))
    else:
        xbc = _linear(hn, weights["xbc"], "conv", conv_tm, rows_per_seq,
                      extra=(jnp.zeros((HALO, D_MODEL), BF16), conv_w, conv_b))
    return dt, qk, v, gz, xbc, gates


def kernel(x, meta_tokens, norm_mix_pre, w_in, conv_w, conv_b, dt_bias, a_log, d_skip, ssd_norm,
           w_ret_branch, w_ssd_branch, w_out, norm_mix_post, norm_ffn_pre, w_gate, w_up, w_down,
           norm_ffn_post):
    b, seq, _ = x.shape
    assert norm_mix_pre.shape[0] == 1, "single-layer block"
    assert meta_tokens.shape[0] == N_META == HALO

    w_in0 = w_in[0].astype(BF16)
    weights = {
        "main": jnp.concatenate([w_in0[:, COL_Q:COL_XBC], w_in0[:, COL_GATES:]], axis=1),
        "xbc": w_in0[:, COL_XBC:COL_DT],
    }
    w_dt1 = w_in0[:, COL_DT:COL_GATES]
    w_dt = jnp.pad(jnp.concatenate([w_dt1, w_dt1], axis=1), ((0, 0), (0, 128 - DT_W)))
    w_ret = w_ret_branch[0].astype(BF16)
    w_ssd = w_ssd_branch[0].astype(BF16)
    expand_mat = jnp.tile(jnp.repeat(jnp.eye(SSD_HEADS, dtype=BF16), SSD_HEADDIM, axis=1), (2, 1))
    dskip_e = jnp.repeat(d_skip[0], SSD_HEADDIM)[None, :]
    gain_pre = norm_mix_pre[0][None, :]
    conv_b2 = conv_b[0][None, :]
    dt_bias2 = jnp.tile(dt_bias[0], 2)[None, :]
    a_log2 = jnp.tile(a_log[0], 2)[None, :]
    ssd_norm2 = ssd_norm[0][None, :]

    half = RET_DK // 2
    inv = ROPE_BASE ** (-jnp.arange(half, dtype=F32) / half)
    ang = jnp.arange(N_META + seq, dtype=F32)[:, None] * inv[None, :]
    cos, sin = jnp.cos(ang), jnp.sin(ang)

    meta = meta_tokens.astype(F32)
    dt_m, qk_m, v_m, _, xbc_m, _ = _input_projection(
        meta, None, gain_pre, w_dt, weights, cos[:N_META], sin[:N_META], conv_w[0], conv_b2, N_META,
        N_META)
    ret_st = _retention(qk_m[None], v_m[None], None, jnp.zeros((RET_HEADS, RET_DK, RET_DV), F32),
                        None, N_META, False)
    ssd_st = _ssd(xbc_m[None], None, dt_m[None], jnp.zeros((SSD_STATE, SSD_DINNER), F32), dt_bias2,
                  a_log2, expand_mat, None, None, None, N_META, 1, False)

    x2d = x.reshape(b * seq, D_MODEL)
    dt, qk, v, gz, xbc, gates = _input_projection(
        x2d, meta, gain_pre, w_dt, weights, cos[N_META:], sin[N_META:], conv_w[0], conv_b2, PROJ_TM,
        CONV_TM)
    as3d = lambda a: a.reshape(b, seq, a.shape[-1])
    branch_a = _retention(as3d(qk), as3d(v), as3d(gz), ret_st, w_ret, RET_T, True)
    branch_b = _ssd(as3d(xbc), as3d(gz), as3d(dt), ssd_st, dt_bias2, a_log2, expand_mat, dskip_e,
                    ssd_norm2, w_ssd, SSD_T, SSD_NSEQ, True)
    out = _tail(x2d, branch_a.reshape(b * seq, D_MODEL), branch_b.reshape(b * seq, D_MODEL), gates,
                w_out[0].astype(BF16), norm_mix_post[0][None, :], norm_ffn_pre[0][None, :],
                w_gate[0].astype(BF16), w_up[0].astype(BF16), w_down[0].astype(BF16),
                norm_ffn_post[0][None, :], TAIL_TM)
    return out.reshape(b, seq, D_MODEL)
```

```python
import functools

import numpy as np
import jax
import jax.numpy as jnp
from jax import lax
from jax.experimental import pallas as pl
from jax.experimental.pallas import tpu as pltpu

F32 = jnp.float32
BF16 = jnp.bfloat16

D_MODEL = 1024
N_META = 16
EPS = 1e-6
RET_HEADS = 4
RET_DK = 256
RET_DV = 512
RET_QK = RET_HEADS * RET_DK
RET_V = RET_HEADS * RET_DV
ROPE_BASE = 10000.0
SSD_DINNER = 2 * D_MODEL
SSD_HEADDIM = 64
SSD_HEADS = SSD_DINNER // SSD_HEADDIM
SSD_GROUPS = 8
SSD_HPG = SSD_HEADS // SSD_GROUPS
SSD_STATE = 128
SSD_CONV = 4
SSD_BC = SSD_GROUPS * SSD_STATE
SSD_XBC = SSD_DINNER + 2 * SSD_BC
SSD_NORM_GROUP = SSD_DINNER // SSD_GROUPS
D_FF = 2816
LOG2E = 1.4426950408889634

COL_Q = 0
COL_K = RET_QK
COL_V = 2 * RET_QK
COL_G = COL_V + RET_V
COL_Z = COL_G + RET_V
COL_XBC = COL_Z + SSD_DINNER
COL_DT = COL_XBC + SSD_XBC
COL_GATES = COL_DT + SSD_HEADS
PROJ_TN = 1024
HALO = 16
LEAD = 32
SEG_PAD = 8
DT_W = 2 * SSD_HEADS

VMEM_LIMIT = 56 * 1024 * 1024


def _dot(a, b):
    return jnp.dot(a, b, preferred_element_type=F32)


def _dot_nt(a, b):
    return lax.dot_general(a, b, (((1,), (1,)), ((), ())), preferred_element_type=F32)


def _dot_tn(a, b):
    return lax.dot_general(a, b, (((0,), (0,)), ((), ())), preferred_element_type=F32)


def _sigmoid(x):
    return 0.5 * jnp.tanh(0.5 * x) + 0.5


def _silu(x):
    h = 0.5 * x
    return h * jnp.tanh(h) + h


def _split2(v):
    hi = v.astype(BF16)
    lo = (v - hi.astype(F32)).astype(BF16)
    return hi, lo


def _split3(v):
    hi = v.astype(BF16)
    r = v - hi.astype(F32)
    mid = r.astype(BF16)
    lo = (r - mid.astype(F32)).astype(BF16)
    return hi, mid, lo


def _rms(v, gain):
    return v * lax.rsqrt(jnp.mean(v * v, axis=-1, keepdims=True) + EPS) * gain


def _params(*sem):
    return pltpu.CompilerParams(dimension_semantics=sem, vmem_limit_bytes=VMEM_LIMIT)


def _lane_blocks(ref_or_val, rows, width):
    return [ref_or_val[rows, c * 128:(c + 1) * 128] for c in range(width // 128)]


def _norm_kernel(*refs, tm, seq_tiles, permute):
    if permute:
        x_ref, g_ref, prev_ref, meta_ref, hnp_ref, stg_ref = refs
    else:
        x_ref, g_ref, hn_ref = refs
    hn_f = _rms(x_ref[...], g_ref[...])
    if not permute:
        hn_ref[...] = hn_f.astype(BF16)
    else:
        seg = tm // 8
        groups = seg // 8
        back = LEAD // 8
        first = (pl.program_id(0) % seq_tiles) == 0
        left_f = _rms(jnp.where(first, meta_ref[N_META - 8:, :], prev_ref[...]), g_ref[...])
        for i in range(8):
            src = left_f if i == 0 else hn_f[i * seg - 8:i * seg, :]
            for c, blk in enumerate(_lane_blocks(src, slice(8 - back, 8), D_MODEL)):
                stg_ref[c, pl.ds(i, back, stride=8), :] = blk
        for m in range(tm // 8):
            i, j0 = m // groups, 8 * (m % groups)
            for c, blk in enumerate(_lane_blocks(hn_f, slice(8 * m, 8 * m + 8), D_MODEL)):
                stg_ref[c, pl.ds(LEAD + 8 * j0 + i, 8, stride=8), :] = blk
        hnp_ref[...] = jnp.concatenate(
            [stg_ref[c] for c in range(D_MODEL // 128)], axis=1).astype(BF16)


def _prenorm(x2d, gain, tm, rows_per_seq, meta=None):
    m = x2d.shape[0]
    permute = meta is not None
    assert m % tm == 0 and rows_per_seq % tm == 0
    row = lambda i: (i, 0)
    const = lambda i: (0, 0)
    in_specs = [pl.BlockSpec((tm, D_MODEL), row), pl.BlockSpec((1, D_MODEL), const)]
    args = [x2d, gain]
    out_specs = pl.BlockSpec((tm, D_MODEL), row)
    out_shape = jax.ShapeDtypeStruct((m, D_MODEL), BF16)
    scratch = []
    if permute:
        assert tm % 64 == 0
        in_specs += [pl.BlockSpec((8, D_MODEL), lambda i: (jnp.maximum(i * (tm // 8) - 1, 0), 0)),
                     pl.BlockSpec((N_META, D_MODEL), const)]
        args += [x2d, meta]
        out_specs = pl.BlockSpec((tm + LEAD, D_MODEL), row)
        out_shape = jax.ShapeDtypeStruct((m // tm * (tm + LEAD), D_MODEL), BF16)
        scratch.append(pltpu.VMEM((D_MODEL // 128, tm + LEAD, 128), F32))
    return pl.pallas_call(
        functools.partial(_norm_kernel, tm=tm, seq_tiles=rows_per_seq // tm, permute=permute),
        grid=(m // tm,),
        in_specs=in_specs,
        out_specs=out_specs,
        out_shape=out_shape,
        scratch_shapes=scratch,
        compiler_params=_params("parallel"),
        name="prenorm",
    )(*args)


def _rotary(acc, cos, sin, scale):
    half = RET_DK // 2
    cos = cos * scale
    sin = sin * scale
    parts = []
    for h in range(RET_HEADS):
        t1 = acc[:, h * RET_DK:h * RET_DK + half]
        t2 = acc[:, h * RET_DK + half:(h + 1) * RET_DK]
        parts += [t1 * cos - t2 * sin, t1 * sin + t2 * cos]
    return jnp.concatenate(parts, axis=1)


MAIN_PLAN = ((("rotary_q", 0, 0), ("rotary_k", 0, 1))
             + tuple(("identity", 1, j) for j in range(RET_V // PROJ_TN))
             + tuple(("silu", 2, j) for j in range((RET_V + SSD_DINNER) // PROJ_TN))
             + tuple(("sigmoid", 3, j) for j in range(2 * D_MODEL // PROJ_TN)))
MAIN_WIDTHS = (2 * RET_QK, RET_V, RET_V + SSD_DINNER, 2 * D_MODEL)


def _main_kernel(x_ref, g_ref, wdt_ref, w_ref, cos_ref, sin_ref, *refs):
    *o_refs, dt_ref, hn_ref = refs
    hn_ref[...] = _rms(x_ref[...], g_ref[...]).astype(BF16)
    dt_ref[...] = _dot(hn_ref[...], wdt_ref[...])[:, :DT_W]
    for jt, (kind, oi, oc) in enumerate(MAIN_PLAN):
        acc = _dot(hn_ref[...], w_ref[:, jt * PROJ_TN:(jt + 1) * PROJ_TN])
        if kind == "rotary_q":
            acc = _rotary(acc, cos_ref[...], sin_ref[...], 1.0)
        elif kind == "rotary_k":
            acc = _rotary(acc, cos_ref[...], sin_ref[...], RET_DK ** -0.5)
        elif kind == "silu":
            acc = _silu(acc)
        elif kind == "sigmoid":
            acc = _sigmoid(acc)
        else:
            assert kind == "identity"
        o_refs[oi][:, oc * PROJ_TN:(oc + 1) * PROJ_TN] = acc.astype(BF16)


def _main_projection(x2d, gain, w_dt, w, cos, sin, tm):
    m = x2d.shape[0]
    rows_per_seq = cos.shape[0]
    assert m % tm == 0 and rows_per_seq % tm == 0 and w.shape[1] == sum(MAIN_WIDTHS)
    seq_tiles = rows_per_seq // tm
    row = lambda i: (i, 0)
    const = lambda i: (0, 0)
    pos = pl.BlockSpec((tm, RET_DK // 2), lambda i: (i % seq_tiles, 0))
    *outs, dt = pl.pallas_call(
        _main_kernel,
        grid=(m // tm,),
        in_specs=[pl.BlockSpec((tm, D_MODEL), row), pl.BlockSpec((1, D_MODEL), const),
                  pl.BlockSpec((D_MODEL, 128), const),
                  pl.BlockSpec(w.shape, const, pipeline_mode=pl.Buffered(1)), pos, pos],
        out_specs=[pl.BlockSpec((tm, n), row) for n in MAIN_WIDTHS] + [pl.BlockSpec((tm, DT_W), row)],
        out_shape=([jax.ShapeDtypeStruct((m, n), BF16) for n in MAIN_WIDTHS]
                   + [jax.ShapeDtypeStruct((m, DT_W), F32)]),
        scratch_shapes=[pltpu.VMEM((tm, D_MODEL), BF16)],
        compiler_params=_params("parallel"),
        name="proj_main",
    )(x2d, gain, w_dt, w, cos, sin)
    return outs, dt


def _lin_kernel(*refs, kind, tm, n, seq_tiles):
    ncol = n // PROJ_TN
    cols = lambda jt: slice(jt * PROJ_TN, (jt + 1) * PROJ_TN)
    if kind == "conv":
        hn_ref, w_ref, halo_ref, meta_ref, cw_ref, cb_ref, o_ref = refs
        first = (pl.program_id(0) % seq_tiles) == 0
        lhs = jnp.concatenate([jnp.where(first, meta_ref[...], halo_ref[...]), hn_ref[...]], axis=0)
        rows = tm + 8
        for jt in range(ncol):
            acc = _dot(lhs, w_ref[:, cols(jt)])
            full = acc[HALO - 8:, :]
            out = cb_ref[:, cols(jt)] + cw_ref[SSD_CONV - 1:SSD_CONV, cols(jt)] * full[8:, :]
            for k in range(SSD_CONV - 1):
                back = SSD_CONV - 1 - k
                win = pltpu.roll(full, rows - (8 - back), axis=0)[:tm, :]
                out = out + cw_ref[k:k + 1, cols(jt)] * win
            o_ref[:, cols(jt)] = _silu(out).astype(BF16)
    elif kind == "conv_interleaved":
        hnp_ref, w_ref, cw_ref, cb_ref, o_ref, stg_ref, acc_ref = refs
        seg = tm // 8
        dyn0 = jnp.minimum(pl.program_id(0), 0)
        for jt in range(ncol):
            slot = jt % 2 + dyn0
            acc_ref[slot] = _dot(hnp_ref[...], w_ref[:, cols(jt)])
            out = cb_ref[:, cols(jt)]
            for back in range(SSD_CONV):
                k = SSD_CONV - 1 - back
                out = out + cw_ref[k:k + 1, cols(jt)] * acc_ref[slot, LEAD - 8 * back:LEAD - 8 * back + tm, :]
            act = _silu(out)
            for j in range(seg):
                for c, blk in enumerate(_lane_blocks(act, slice(8 * j, 8 * j + 8), PROJ_TN)):
                    stg_ref[jt % 2, c, pl.ds(j, 8, stride=seg + SEG_PAD), :] = blk
            for i in range(8):
                rows = slice((seg + SEG_PAD) * i, (seg + SEG_PAD) * i + seg)
                o_ref[seg * i:seg * (i + 1), cols(jt)] = jnp.concatenate(
                    [stg_ref[jt % 2, c, rows, :] for c in range(PROJ_TN // 128)], axis=1).astype(BF16)
    else:
        raise ValueError(kind)


def _linear(hn, w, kind, tm, rows_per_seq, extra=()):
    interleaved = kind == "conv_interleaved"
    rows_in = tm + LEAD if interleaved else tm
    assert hn.shape[0] % rows_in == 0
    m = hn.shape[0] // rows_in * tm
    n = w.shape[1]
    assert rows_per_seq % tm == 0 and n % PROJ_TN == 0 and tm % HALO == 0
    seq_tiles = rows_per_seq // tm
    const = lambda i: (0, 0)
    resident = functools.partial(pl.BlockSpec, index_map=const, pipeline_mode=pl.Buffered(1))
    in_specs = [pl.BlockSpec((rows_in, D_MODEL), lambda i: (i, 0)), resident((D_MODEL, n))]
    args = [hn, w]
    if kind == "conv":
        meta_hn, conv_w, conv_b = extra
        in_specs += [
            pl.BlockSpec((HALO, D_MODEL), lambda i: (jnp.maximum(i * (tm // HALO) - 1, 0), 0)),
            pl.BlockSpec((HALO, D_MODEL), const),
            pl.BlockSpec((SSD_CONV, n), const),
            pl.BlockSpec((1, n), const),
        ]
        args += [hn, meta_hn, conv_w, conv_b]
    elif interleaved:
        in_specs += [pl.BlockSpec((SSD_CONV, n), const), pl.BlockSpec((1, n), const)]
        args += list(extra)
    scratch = []
    if interleaved:
        assert tm % 64 == 0
        scratch.append(pltpu.VMEM((2, PROJ_TN // 128, 8 * (tm // 8 + SEG_PAD), 128), F32))
        scratch.append(pltpu.VMEM((2, tm + LEAD, PROJ_TN), F32))
    return pl.pallas_call(
        functools.partial(_lin_kernel, kind=kind, tm=tm, n=n, seq_tiles=seq_tiles),
        grid=(m // tm,),
        in_specs=in_specs,
        out_specs=pl.BlockSpec((tm, n), lambda i: (i, 0)),
        out_shape=jax.ShapeDtypeStruct((m, n), BF16),
        scratch_shapes=scratch,
        compiler_params=_params("parallel"),
        name="proj_" + kind,
    )(*args)


def _ret_tables(t):
    log_g = np.log1p(-np.exp2(-5.0 - np.arange(RET_HEADS, dtype=np.float64)))
    idx = np.arange(t, dtype=np.float64)
    diff = idx[:, None] - idx[None, :]
    dmask = np.where(diff >= 0, np.exp(log_g[:, None, None] * np.maximum(diff, 0.0)), 0.0)
    xi = np.exp(log_g[:, None] * (idx[None, :] + 1.0))
    zeta = np.exp(log_g[:, None] * (t - 1.0 - idx[None, :]))
    g_chunk = np.exp(log_g * t)
    xi = np.broadcast_to(xi[:, :, None], (RET_HEADS, t, RET_DK))
    zeta = np.broadcast_to(zeta[:, :, None], (RET_HEADS, t, RET_DK))
    return (jnp.asarray(dmask, F32), jnp.asarray(xi, F32), jnp.asarray(zeta, F32),
            tuple(float(v) for v in g_chunk))


def _ret_kernel(*refs, t, nseq, g_chunk, emit_out):
    if emit_out:
        (q_ref, k_ref, v_ref, st0_ref, dmask_ref, xi_ref, zeta_ref, g_ref, w_ref,
         o_ref, st_ref, ret_ref) = refs
    else:
        (q_ref, k_ref, v_ref, st0_ref, dmask_ref, xi_ref, zeta_ref, stout_ref, st_ref) = refs
    c = pl.program_id(1)

    @pl.when(c == 0)
    def _():
        for s in range(nseq):
            st_ref[s] = st0_ref[...]

    for h in range(RET_HEADS):
        for s in range(nseq):
            qh = q_ref[s, :, h * RET_DK:(h + 1) * RET_DK]
            kh = k_ref[s, :, h * RET_DK:(h + 1) * RET_DK]
            vh = v_ref[s, :, h * RET_DV:(h + 1) * RET_DV]
            sc = _dot_nt(qh, kh) * dmask_ref[h]
            qx = (qh.astype(F32) * xi_ref[h]).astype(BF16)
            kz = (kh.astype(F32) * zeta_ref[h]).astype(BF16)
            st = st_ref[s, h]
            if emit_out:
                out = _dot(sc.astype(BF16), vh) + _dot(qx, st.astype(BF16))
                r = out * lax.rsqrt(jnp.mean(out * out, axis=-1, keepdims=True) + EPS)
                gate = g_ref[s, :, h * RET_DV:(h + 1) * RET_DV].astype(F32)
                ret_ref[s * t:(s + 1) * t, h * RET_DV:(h + 1) * RET_DV] = (r * gate).astype(BF16)
            st_ref[s, h] = st * g_chunk[h] + _dot_tn(kz, vh)

    if emit_out:
        out = _dot(ret_ref[...], w_ref[...]).astype(BF16)
        for s in range(nseq):
            o_ref[s] = out[s * t:(s + 1) * t, :]
    else:
        @pl.when(c == pl.num_programs(1) - 1)
        def _():
            stout_ref[...] = st_ref[0]


def _retention(qk3d, v3d, gz3d, st0, w_ret, t, nseq, emit_out):
    b, l, _ = qk3d.shape
    assert l % t == 0 and b % nseq == 0
    nc = l // t
    dmask, xi, zeta, g_chunk = _ret_tables(t)
    const3 = lambda bb, cc: (0, 0, 0)
    in_specs = [
        pl.BlockSpec((nseq, t, RET_QK), lambda bb, cc: (bb, cc, 0)),
        pl.BlockSpec((nseq, t, RET_QK), lambda bb, cc: (bb, cc, 1)),
        pl.BlockSpec((nseq, t, RET_V), lambda bb, cc: (bb, cc, 0)),
        pl.BlockSpec((RET_HEADS, RET_DK, RET_DV), const3),
        pl.BlockSpec((RET_HEADS, t, t), const3),
        pl.BlockSpec((RET_HEADS, t, RET_DK), const3),
        pl.BlockSpec((RET_HEADS, t, RET_DK), const3),
    ]
    args = [qk3d, qk3d, v3d, st0, dmask, xi, zeta]
    scratch = [pltpu.VMEM((nseq, RET_HEADS, RET_DK, RET_DV), F32)]
    if emit_out:
        in_specs += [
            pl.BlockSpec((nseq, t, RET_V), lambda bb, cc: (bb, cc, 0)),
            pl.BlockSpec((RET_V, D_MODEL), lambda bb, cc: (0, 0)),
        ]
        args += [gz3d, w_ret]
        out_specs = pl.BlockSpec((nseq, t, D_MODEL), lambda bb, cc: (bb, cc, 0))
        out_shape = jax.ShapeDtypeStruct((b, l, D_MODEL), BF16)
        scratch.append(pltpu.VMEM((nseq * t, RET_V), BF16))
    else:
        assert b == 1 and nseq == 1
        out_specs = pl.BlockSpec((RET_HEADS, RET_DK, RET_DV), const3)
        out_shape = jax.ShapeDtypeStruct((RET_HEADS, RET_DK, RET_DV), F32)
    return pl.pallas_call(
        functools.partial(_ret_kernel, t=t, nseq=nseq, g_chunk=g_chunk, emit_out=emit_out),
        grid=(b // nseq, nc),
        in_specs=in_specs,
        out_specs=out_specs,
        out_shape=out_shape,
        scratch_shapes=scratch,
        compiler_params=_params("parallel", "arbitrary"),
        name="retention" if emit_out else "retention_meta",
    )(*args)


def _ssd_kernel(*refs, t, nseq, emit_out):
    if emit_out:
        (xbc_ref, dt_ref, st0_ref, dtb_ref, alog_ref, e_ref,
         z_ref, dskip_ref, norm_ref, w_ref, o_ref, st_ref, y_ref) = refs
    else:
        (xbc_ref, dt_ref, st0_ref, dtb_ref, alog_ref, e_ref, stout_ref, st_ref) = refs
    c = pl.program_id(1)

    @pl.when(c == 0)
    def _():
        for q in range(nseq):
            st_ref[q] = st0_ref[...]

    row = lax.broadcasted_iota(jnp.int32, (t, t), 0)
    col = lax.broadcasted_iota(jnp.int32, (t, t), 1)
    causal = row >= col
    tril = jnp.where(causal, 1.0, 0.0).astype(BF16)
    lane = lax.broadcasted_iota(jnp.int32, (t, DT_W), 1)
    head_of_col = lax.broadcasted_iota(jnp.int32, (t, SSD_NORM_GROUP), 1) // SSD_HEADDIM
    zeros_b = jnp.zeros((t, SSD_STATE), BF16)

    for q in range(nseq):
        dt_raw = dt_ref[q] + dtb_ref[...]
        dt = jnp.maximum(dt_raw, 0.0) + jnp.log1p(jnp.exp(-jnp.abs(dt_raw)))
        da = dt * (-jnp.exp(alog_ref[...]))
        cs = sum(_dot(tril, part) for part in _split3(da))
        ecs = jnp.exp(cs)
        edl = jnp.exp(cs[t - 1:t, :] - cs)
        cs2 = cs * LOG2E
        cs2_t = cs2.T

        def expand(v):
            hi, lo = _split2(v)
            return _dot(jnp.where(lane < SSD_HEADS, hi, lo), e_ref[...])

        xs = xbc_ref[q, :, :SSD_DINNER].astype(F32)
        xdt_f = xs * expand(dt)
        xdt = xdt_f.astype(BF16)
        xdec = (xdt_f * expand(edl)).astype(BF16)
        ecs_e = expand(ecs)

        def b_cols(g):
            return xbc_ref[q, :, SSD_DINNER + g * SSD_STATE:SSD_DINNER + (g + 1) * SSD_STATE]

        for pair in range(SSD_GROUPS // 2):
            if emit_out:
                c2 = xbc_ref[q, :, SSD_DINNER + SSD_BC + 2 * pair * SSD_STATE:
                             SSD_DINNER + SSD_BC + (2 * pair + 2) * SSD_STATE]
                b2 = jnp.concatenate([jnp.concatenate([b_cols(2 * pair), zeros_b], axis=1),
                                      jnp.concatenate([zeros_b, b_cols(2 * pair + 1)], axis=1)], axis=0)
                cb2 = _dot_nt(c2, b2)
            for half in range(2):
                g = 2 * pair + half
                gs = slice(g * SSD_NORM_GROUP, (g + 1) * SSD_NORM_GROUP)
                bg = b_cols(g)
                stg = st_ref[q, :, gs]
                if emit_out:
                    cg = c2[:, half * SSD_STATE:(half + 1) * SSD_STATE]
                    cb = jnp.where(causal, cb2[:, half * t:(half + 1) * t], 0.0)
                    xg = xdt[:, gs]
                    ms, xm = [], []
                    for h in range(SSD_HPG):
                        hh = g * SSD_HPG + h
                        seg = jnp.minimum(cs2[:, hh:hh + 1] - cs2_t[hh:hh + 1, :], 0.0)
                        ms.append((cb * jnp.exp2(seg)).astype(BF16))
                        xm.append(jnp.where(head_of_col == h, xg, jnp.zeros_like(xg)))
                    y = _dot(jnp.concatenate(ms, axis=1), jnp.concatenate(xm, axis=0))
                    y = y + _dot(cg, stg.astype(BF16)) * ecs_e[:, gs]
                    y = y + dskip_ref[:, gs] * xs[:, gs]
                    y = y * z_ref[q, :, gs].astype(F32)
                    y = y * lax.rsqrt(jnp.mean(y * y, axis=-1, keepdims=True) + EPS)
                    y_ref[q * t:(q + 1) * t, gs] = (y * norm_ref[:, gs]).astype(BF16)
                st_ref[q, :, gs] = stg * ecs_e[t - 1:t, gs] + _dot_tn(bg, xdec[:, gs])

    if emit_out:
        out = _dot(y_ref[...], w_ref[...]).astype(BF16)
        for q in range(nseq):
            o_ref[q] = out[q * t:(q + 1) * t, :]
    else:
        @pl.when(c == pl.num_programs(1) - 1)
        def _():
            stout_ref[...] = st_ref[0]


def _ssd(xbc3d, gz3d, dt3d, st0, dt_bias, a_log, expand_mat, dskip_e, ssd_norm, w_ssd, t, nseq, emit_out):
    b, l, _ = xbc3d.shape
    assert l % t == 0 and b % nseq == 0
    nc = l // t
    const2 = lambda bb, cc: (0, 0)
    in_specs = [
        pl.BlockSpec((nseq, t, SSD_XBC), lambda bb, cc: (bb, cc, 0)),
        pl.BlockSpec((nseq, t, DT_W), lambda bb, cc: (bb, cc, 0)),
        pl.BlockSpec((SSD_STATE, SSD_DINNER), const2),
        pl.BlockSpec((1, DT_W), const2),
        pl.BlockSpec((1, DT_W), const2),
        pl.BlockSpec((DT_W, SSD_DINNER), const2),
    ]
    args = [xbc3d, dt3d, st0, dt_bias, a_log, expand_mat]
    scratch = [pltpu.VMEM((nseq, SSD_STATE, SSD_DINNER), F32)]
    if emit_out:
        in_specs += [
            pl.BlockSpec((nseq, t, SSD_DINNER), lambda bb, cc: (bb, cc, RET_V // SSD_DINNER)),
            pl.BlockSpec((1, SSD_DINNER), const2),
            pl.BlockSpec((1, SSD_DINNER), const2),
            pl.BlockSpec((SSD_DINNER, D_MODEL), const2),
        ]
        args += [gz3d, dskip_e, ssd_norm, w_ssd]
        out_specs = pl.BlockSpec((nseq, t, D_MODEL), lambda bb, cc: (bb, cc, 0))
        out_shape = jax.ShapeDtypeStruct((b, l, D_MODEL), BF16)
        scratch.append(pltpu.VMEM((nseq * t, SSD_DINNER), BF16))
    else:
        assert b == 1 and nseq == 1
        out_specs = pl.BlockSpec((SSD_STATE, SSD_DINNER), const2)
        out_shape = jax.ShapeDtypeStruct((SSD_STATE, SSD_DINNER), F32)
    return pl.pallas_call(
        functools.partial(_ssd_kernel, t=t, nseq=nseq, emit_out=emit_out),
        grid=(b // nseq, nc),
        in_specs=in_specs,
        out_specs=out_specs,
        out_shape=out_shape,
        scratch_shapes=scratch,
        compiler_params=_params("parallel", "arbitrary"),
        name="ssd" if emit_out else "ssd_meta",
    )(*args)


def _tail_kernel(x_ref, ba_ref, bb_ref, gates_ref, wout_ref, npost_ref, nfpre_ref, wg_ref, wu_ref,
                 wd_ref, nfpost_ref, o_ref):
    rb = x_ref.shape[0] // TAIL_SPLIT
    blocks = [slice(r * rb, (r + 1) * rb) for r in range(TAIL_SPLIT)]
    merged = [(gates_ref[rows, :D_MODEL].astype(F32) * ba_ref[rows, :].astype(F32)
               + gates_ref[rows, D_MODEL:].astype(F32) * bb_ref[rows, :].astype(F32)).astype(BF16)
              for rows in blocks]
    mix = [_dot(m, wout_ref[...]) for m in merged]
    h1 = [x_ref[rows, :] + _rms(mx, npost_ref[...]) for rows, mx in zip(blocks, mix)]
    u = [_rms(h, nfpre_ref[...]).astype(BF16) for h in h1]
    gate = [_dot(v, wg_ref[...]) for v in u]
    up = [_dot(v, wu_ref[...]) for v in u]
    act = [(_silu(g) * p).astype(BF16) for g, p in zip(gate, up)]
    f = [_dot(a, wd_ref[...]) for a in act]
    for rows, h, ff in zip(blocks, h1, f):
        o_ref[rows, :] = h + _rms(ff, nfpost_ref[...])


def _tail(x2d, ba, bb, gates, w_out, n_post, nf_pre, w_gate, w_up, w_down, nf_post, tm):
    m = x2d.shape[0]
    assert m % tm == 0
    row = lambda i: (i, 0)
    const = lambda i: (0, 0)
    resident = functools.partial(pl.BlockSpec, index_map=const, pipeline_mode=pl.Buffered(1))
    return pl.pallas_call(
        _tail_kernel,
        grid=(m // tm,),
        in_specs=[
            pl.BlockSpec((tm, D_MODEL), row),
            pl.BlockSpec((tm, D_MODEL), row),
            pl.BlockSpec((tm, D_MODEL), row),
            pl.BlockSpec((tm, 2 * D_MODEL), row),
            resident((D_MODEL, D_MODEL)),
            pl.BlockSpec((1, D_MODEL), const),
            pl.BlockSpec((1, D_MODEL), const),
            resident((D_MODEL, D_FF)),
            resident((D_MODEL, D_FF)),
            resident((D_FF, D_MODEL)),
            pl.BlockSpec((1, D_MODEL), const),
        ],
        out_specs=pl.BlockSpec((tm, D_MODEL), row),
        out_shape=jax.ShapeDtypeStruct((m, D_MODEL), F32),
        compiler_params=_params("parallel"),
        name="tail",
    )(x2d, ba, bb, gates, w_out, n_post, nf_pre, w_gate, w_up, w_down, nf_post)


PROJ_TM = 512
CONV_TM = 1024
RET_T = 256
RET_NSEQ = 2
SSD_T = 128
SSD_NSEQ = 4
TAIL_TM = 512
TAIL_SPLIT = 2


def _input_projection(x2d, meta, gain, w_dt, weights, cos, sin, conv_w, conv_b, tm, conv_tm):
    rows_per_seq = cos.shape[0]
    interleave = meta is not None
    hn = _prenorm(x2d, gain, conv_tm, rows_per_seq, meta)
    (qk, v, gz, gates), dt = _main_projection(x2d, gain, w_dt, weights["main"], cos, sin, tm)
    if interleave:
        xbc = _linear(hn, weights["xbc"], "conv_interleaved", conv_tm, rows_per_seq,
                      extra=(conv_w, conv_b))
    else:
        xbc = _linear(hn, weights["xbc"], "conv", conv_tm, rows_per_seq,
                      extra=(jnp.zeros((HALO, D_MODEL), BF16), conv_w, conv_b))
    return dt, qk, v, gz, xbc, gates


def kernel(x, meta_tokens, norm_mix_pre, w_in, conv_w, conv_b, dt_bias, a_log, d_skip, ssd_norm,
           w_ret_branch, w_ssd_branch, w_out, norm_mix_post, norm_ffn_pre, w_gate, w_up, w_down,
           norm_ffn_post):
    b, seq, _ = x.shape
    assert norm_mix_pre.shape[0] == 1, "single-layer block"
    assert meta_tokens.shape[0] == N_META == HALO

    w_in0 = w_in[0].astype(BF16)
    weights = {
        "main": jnp.concatenate([w_in0[:, COL_Q:COL_XBC], w_in0[:, COL_GATES:]], axis=1),
        "xbc": w_in0[:, COL_XBC:COL_DT],
    }
    w_dt1 = w_in0[:, COL_DT:COL_GATES]
    w_dt = jnp.pad(jnp.concatenate([w_dt1, w_dt1], axis=1), ((0, 0), (0, 128 - DT_W)))
    w_ret = w_ret_branch[0].astype(BF16)
    w_ssd = w_ssd_branch[0].astype(BF16)
    expand_mat = jnp.tile(jnp.repeat(jnp.eye(SSD_HEADS, dtype=BF16), SSD_HEADDIM, axis=1), (2, 1))
    dskip_e = jnp.repeat(d_skip[0], SSD_HEADDIM)[None, :]
    gain_pre = norm_mix_pre[0][None, :]
    conv_b2 = conv_b[0][None, :]
    dt_bias2 = jnp.tile(dt_bias[0], 2)[None, :]
    a_log2 = jnp.tile(a_log[0], 2)[None, :]
    ssd_norm2 = ssd_norm[0][None, :]

    half = RET_DK // 2
    inv = ROPE_BASE ** (-jnp.arange(half, dtype=F32) / half)
    ang = jnp.arange(N_META + seq, dtype=F32)[:, None] * inv[None, :]
    cos, sin = jnp.cos(ang), jnp.sin(ang)

    meta = meta_tokens.astype(F32)
    dt_m, qk_m, v_m, _, xbc_m, _ = _input_projection(
        meta, None, gain_pre, w_dt, weights, cos[:N_META], sin[:N_META], conv_w[0], conv_b2, N_META,
        N_META)
    ret_st = _retention(qk_m[None], v_m[None], None, jnp.zeros((RET_HEADS, RET_DK, RET_DV), F32),
                        None, N_META, 1, False)
    ssd_st = _ssd(xbc_m[None], None, dt_m[None], jnp.zeros((SSD_STATE, SSD_DINNER), F32), dt_bias2,
                  a_log2, expand_mat, None, None, None, N_META, 1, False)

    x2d = x.reshape(b * seq, D_MODEL)
    dt, qk, v, gz, xbc, gates = _input_projection(
        x2d, meta, gain_pre, w_dt, weights, cos[N_META:], sin[N_META:], conv_w[0], conv_b2, PROJ_TM,
        CONV_TM)
    as3d = lambda a: a.reshape(b, seq, a.shape[-1])
    branch_a = _retention(as3d(qk), as3d(v), as3d(gz), ret_st, w_ret, RET_T, RET_NSEQ, True)
    branch_b = _ssd(as3d(xbc), as3d(gz), as3d(dt), ssd_st, dt_bias2, a_log2, expand_mat, dskip_e,
                    ssd_norm2, w_ssd, SSD_T, SSD_NSEQ, True)
    out = _tail(x2d, branch_a.reshape(b * seq, D_MODEL), branch_b.reshape(b * seq, D_MODEL), gates,
                w_out[0].astype(BF16), norm_mix_post[0][None, :], norm_ffn_pre[0][None, :],
                w_gate[0].astype(BF16), w_up[0].astype(BF16), w_down[0].astype(BF16),
                norm_ffn_post[0][None, :], TAIL_TM)
    return out.reshape(b, seq, D_MODEL)
```
